```python
import math
import jax, jax.numpy as jnp
from jax import lax
import numpy as np

D_MODEL = 2048
BATCH = 2
SEQ = 4096
DEPTH = 4
DEC_BATCH = 8
DEC_SEQ = 8
PAST_LEN = 16384
PAGE_SIZE = 128

N_MIXERS = 3
N_LAYERS_A = (DEPTH + 2) // 3
N_LAYERS_B = (DEPTH + 1) // 3
N_LAYERS_C = DEPTH // 3
N_DENSE = (DEPTH + 1) // 2
N_MOE = DEPTH // 2

CHUNK = 128
E_A = D_MODEL
H_A = 16
G_A = E_A // H_A

H_B = 16
DH_B = D_MODEL // H_B
Q_BLOCK = 128
FORGET_BIAS = 4.0

CONV_W = 3
E_C = D_MODEL

D_FF = 5632
N_EXPERTS = 8
TOP_K = 2
D_FF_E = 7168

ALPHA = (2.0 * DEPTH) ** 0.25
BETA = (8.0 * DEPTH) ** -0.25
LN_EPS = 1e-5
NEG_INF = -1e30

kernel_name = 'hybrid_chunkmlp_fox_shortconv_moe_decode_step'


def layer_norm(x, g, b):
    xf = x.astype(jnp.float32)
    mu = xf.mean(-1, keepdims=True)
    var = jnp.square(xf - mu).mean(-1, keepdims=True)
    return ((xf - mu) * lax.rsqrt(var + LN_EPS) * g.astype(jnp.float32) + b.astype(jnp.float32)).astype(x.dtype)


def chunk_mlp_mixer(x, w_in, ln_g, ln_b, w_s, b_s, w_out):
    n, L, _ = x.shape
    u, v = jnp.split(jax.nn.gelu(x @ w_in), 2, axis=-1)
    v = layer_norm(v, ln_g, ln_b)
    n_chunks = -(-L // CHUNK)
    vp = jnp.pad(v, ((0, 0), (0, n_chunks * CHUNK - L), (0, 0)))
    vp = vp.reshape(n, n_chunks, CHUNK, H_A, G_A)
    causal = jnp.tril(jnp.ones((CHUNK, CHUNK), dtype=bool))
    w = jnp.where(causal[None], w_s, 0)
    s = jnp.einsum('hts,bcshg->bcthg', w, vp) + b_s.T[:, :, None]
    s = s.reshape(n, n_chunks * CHUNK, E_A)[:, :L]
    return (u * s) @ w_out, v


def fox_project(x, w_in, b_f):
    n, L, _ = x.shape
    proj = x @ w_in
    qkv = proj[..., :3 * H_B * DH_B].reshape(n, L, 3, H_B, DH_B)
    logf = jax.nn.log_sigmoid((proj[..., 3 * H_B * DH_B:] + b_f).astype(jnp.float32))
    return qkv[:, :, 0], qkv[:, :, 1], qkv[:, :, 2], logf


def fox_attend(q, k, v, cq, ck, mask):
    s = jnp.einsum('bqhd,bkhd->bhqk', q, k).astype(jnp.float32) * (DH_B ** -0.5)
    s = s + jnp.transpose(cq, (0, 2, 1))[..., None] - jnp.transpose(ck, (0, 2, 1))[:, :, None, :]
    s = jnp.where(mask, s, NEG_INF)
    p = jax.nn.softmax(s, axis=-1).astype(v.dtype)
    return jnp.einsum('bhqk,bkhd->bqhd', p, v)


def fox_prompt(x, w_in, b_f, w_o):
    n, L, _ = x.shape
    q, k, v, logf = fox_project(x, w_in, b_f)
    c = lax.cumsum(logf, axis=1)
    key_pos = jnp.arange(L)

    def block(i):
        start = i * Q_BLOCK
        qi = lax.dynamic_slice_in_dim(q, start, Q_BLOCK, axis=1)
        ci = lax.dynamic_slice_in_dim(c, start, Q_BLOCK, axis=1)
        mask = (start + jnp.arange(Q_BLOCK))[:, None] >= key_pos[None, :]
        return fox_attend(qi, k, v, ci, c, mask)

    o = lax.map(block, jnp.arange(L // Q_BLOCK))
    o = jnp.transpose(o, (1, 0, 2, 3, 4)).reshape(n, L, H_B * DH_B)
    return o @ w_o, k, v, logf


def fox_sample(x, cache_k, cache_v, cache_logf, layer, page_table, w_in, b_f, w_o):
    n, L, _ = x.shape
    q, k, v, logf = fox_project(x, w_in, b_f)
    past = page_table.shape[1] * PAGE_SIZE
    k_past = cache_k[layer, page_table].reshape(n, past, H_B, DH_B)
    v_past = cache_v[layer, page_table].reshape(n, past, H_B, DH_B)
    lf_past = cache_logf[layer, page_table].reshape(n, past, H_B).astype(jnp.float32)
    c_new = lax.cumsum(logf, axis=1)
    suffix = lax.cumsum(lf_past, axis=1, reverse=True) - lf_past
    ck = jnp.concatenate([-suffix, c_new], axis=1)
    k_all = jnp.concatenate([k_past.astype(k.dtype), k], axis=1)
    v_all = jnp.concatenate([v_past.astype(v.dtype), v], axis=1)
    mask = jnp.concatenate([jnp.ones((L, past), dtype=bool), jnp.tril(jnp.ones((L, L), dtype=bool))], axis=1)
    o = fox_attend(q, k_all, v_all, c_new, ck, mask).reshape(n, L, H_B * DH_B)
    return o @ w_o, k, v, logf


def short_conv_mixer(x, z_prev, w_in, conv_w, w_out):
    L = x.shape[1]
    b_gate, c_gate, h = jnp.split(x @ w_in, 3, axis=-1)
    z = c_gate * h
    zp = jnp.concatenate([z_prev.astype(z.dtype), z], axis=1)
    y = sum(conv_w[j] * zp[:, j:j + L] for j in range(CONV_W))
    return (b_gate * y) @ w_out, zp[:, L:]


def swiglu(x, w_gate, w_up, w_down):
    return (jax.nn.silu(x @ w_gate) * (x @ w_up)) @ w_down


def moe_swiglu(x, w_router, b_router, w_gate, w_up, w_down):
    shp = x.shape
    xt = x.reshape(-1, shp[-1])
    logits = (xt @ w_router + b_router).astype(jnp.float32)
    top_val, top_idx = lax.top_k(logits, TOP_K)
    gates = jax.nn.softmax(top_val, axis=-1)
    combine = jnp.einsum('tk,tke->te', gates, jax.nn.one_hot(top_idx, N_EXPERTS, dtype=jnp.float32)).astype(x.dtype)
    out = jnp.zeros_like(xt)
    for e in range(N_EXPERTS):
        out = out + combine[:, e:e + 1] * swiglu(xt, w_gate[e], w_up[e], w_down[e])
    return out.reshape(shp)


def setup_inputs(seed: int = 0) -> dict:
    key = jax.random.key(seed)
    ks = iter(jax.random.split(key, 40))

    def nrm(shape, scale):
        return jax.random.normal(next(ks), shape, jnp.float32) * scale

    n_pages = PAST_LEN // PAGE_SIZE
    n_pool = (DEC_BATCH * n_pages * 5) // 4
    page_table = jax.random.permutation(next(ks), n_pool)[:DEC_BATCH * n_pages]
    page_table = page_table.reshape(DEC_BATCH, n_pages).astype(jnp.int32)
    D = D_MODEL
    return {
        'x_prompt': nrm((BATCH, SEQ, D), 1.0),
        'x_sample': nrm((DEC_BATCH, DEC_SEQ, D), 1.0),
        'cache_k': nrm((N_LAYERS_B, n_pool, PAGE_SIZE, H_B, DH_B), 1.0),
        'cache_v': nrm((N_LAYERS_B, n_pool, PAGE_SIZE, H_B, DH_B), 1.0),
        'cache_logf': jax.nn.log_sigmoid(FORGET_BIAS + nrm((N_LAYERS_B, n_pool, PAGE_SIZE, H_B), 1.0)),
        'state_conv': nrm((N_LAYERS_C, DEC_BATCH, CONV_W - 1, E_C), 1.0),
        'page_table': page_table,
        'a_w_in': nrm((N_LAYERS_A, D, 2 * E_A), D ** -0.5),
        'a_ln_g': 1.0 + nrm((N_LAYERS_A, E_A), 0.02),
        'a_ln_b': nrm((N_LAYERS_A, E_A), 0.02),
        'a_w_s': nrm((N_LAYERS_A, H_A, CHUNK, CHUNK), CHUNK ** -0.5),
        'a_b_s': 1.0 + nrm((N_LAYERS_A, H_A, CHUNK), 0.02),
        'a_w_out': nrm((N_LAYERS_A, E_A, D), E_A ** -0.5 * BETA),
        'b_w_in': nrm((N_LAYERS_B, D, 3 * H_B * DH_B + H_B), D ** -0.5),
        'b_b_f': FORGET_BIAS + nrm((N_LAYERS_B, H_B), 0.5),
        'b_w_o': nrm((N_LAYERS_B, H_B * DH_B, D), (H_B * DH_B) ** -0.5 * BETA),
        'c_w_in': nrm((N_LAYERS_C, D, 3 * E_C), D ** -0.5),
        'c_conv_w': nrm((N_LAYERS_C, CONV_W, E_C), CONV_W ** -0.5),
        'c_w_out': nrm((N_LAYERS_C, E_C, D), E_C ** -0.5 * BETA),
        'ffn_w_gate': nrm((N_DENSE, D, D_FF), D ** -0.5),
        'ffn_w_up': nrm((N_DENSE, D, D_FF), D ** -0.5),
        'ffn_w_down': nrm((N_DENSE, D_FF, D), D_FF ** -0.5 * BETA),
        'moe_w_router': nrm((N_MOE, D, N_EXPERTS), D ** -0.5),
        'moe_b_router': nrm((N_MOE, N_EXPERTS), 0.01),
        'moe_w_gate': nrm((N_MOE, N_EXPERTS, D, D_FF_E), D ** -0.5),
        'moe_w_up': nrm((N_MOE, N_EXPERTS, D, D_FF_E), D ** -0.5),
        'moe_w_down': nrm((N_MOE, N_EXPERTS, D_FF_E, D), D_FF_E ** -0.5 * BETA),
        'ln1_g': 1.0 + nrm((DEPTH, D), 0.02),
        'ln1_b': nrm((DEPTH, D), 0.02),
        'ln2_g': 1.0 + nrm((DEPTH, D), 0.02),
        'ln2_b': nrm((DEPTH, D), 0.02),
    }


def reference(x_prompt, x_sample, cache_k, cache_v, cache_logf, state_conv, page_table,
              a_w_in, a_ln_g, a_ln_b, a_w_s, a_b_s, a_w_out,
              b_w_in, b_b_f, b_w_o,
              c_w_in, c_conv_w, c_w_out,
              ffn_w_gate, ffn_w_up, ffn_w_down,
              moe_w_router, moe_b_router, moe_w_gate, moe_w_up, moe_w_down,
              ln1_g, ln1_b, ln2_g, ln2_b):
    xp, xs = x_prompt, x_sample
    k_p, v_p, lf_p, k_s, v_s, lf_s = [], [], [], [], [], []
    conv_p, conv_s, chunk_v_s = [], [], []
    for i in range(DEPTH):
        kind = i % N_MIXERS
        j = i // N_MIXERS
        if kind == 0:
            mp, _ = chunk_mlp_mixer(xp, a_w_in[j], a_ln_g[j], a_ln_b[j], a_w_s[j], a_b_s[j], a_w_out[j])
            ms, v_rows = chunk_mlp_mixer(xs, a_w_in[j], a_ln_g[j], a_ln_b[j], a_w_s[j], a_b_s[j], a_w_out[j])
            chunk_v_s.append(v_rows)
        elif kind == 1:
            mp, kr, vr, lr = fox_prompt(xp, b_w_in[j], b_b_f[j], b_w_o[j])
            ms, kn, vn, ln_ = fox_sample(xs, cache_k, cache_v, cache_logf, j, page_table, b_w_in[j], b_b_f[j], b_w_o[j])
            k_p.append(kr); v_p.append(vr); lf_p.append(lr)
            k_s.append(kn); v_s.append(vn); lf_s.append(ln_)
        else:
            z0 = jnp.zeros((xp.shape[0], CONV_W - 1, E_C), dtype=xp.dtype)
            mp, cp = short_conv_mixer(xp, z0, c_w_in[j], c_conv_w[j], c_w_out[j])
            ms, cs = short_conv_mixer(xs, state_conv[j], c_w_in[j], c_conv_w[j], c_w_out[j])
            conv_p.append(cp); conv_s.append(cs)
        xp = layer_norm(ALPHA * xp + mp, ln1_g[i], ln1_b[i])
        xs = layer_norm(ALPHA * xs + ms, ln1_g[i], ln1_b[i])
        f = i // 2
        if i % 2 == 0:
            fp = swiglu(xp, ffn_w_gate[f], ffn_w_up[f], ffn_w_down[f])
            fs = swiglu(xs, ffn_w_gate[f], ffn_w_up[f], ffn_w_down[f])
        else:
            fp = moe_swiglu(xp, moe_w_router[f], moe_b_router[f], moe_w_gate[f], moe_w_up[f], moe_w_down[f])
            fs = moe_swiglu(xs, moe_w_router[f], moe_b_router[f], moe_w_gate[f], moe_w_up[f], moe_w_down[f])
        xp = layer_norm(ALPHA * xp + fp, ln2_g[i], ln2_b[i])
        xs = layer_norm(ALPHA * xs + fs, ln2_g[i], ln2_b[i])
    return (xp, xs, jnp.stack(k_p), jnp.stack(v_p), jnp.stack(lf_p), jnp.stack(k_s), jnp.stack(v_s), jnp.stack(lf_s), jnp.stack(conv_p), jnp.stack(conv_s), jnp.stack(chunk_v_s))
```

```python
import functools

import jax
import jax.numpy as jnp
from jax import lax
from jax.experimental import pallas as pl
from jax.experimental.pallas import tpu as pltpu

F32 = jnp.float32
BF16 = jnp.bfloat16

H_A = 16
CHUNK = 128
H_B = 16
PAGE_SIZE = 128
CONV_W = 3
TOP_K = 2
LN_EPS = 1e-5
NEG_INF = -1e30

TM = 512
TN = 512
TK = 512
TKV = 512
PAGES_PER_STEP = 4
Q_ROWS = 16

VMEM_LIMIT = 56 * 1024 * 1024


def _cparams(*sem):
    return pltpu.CompilerParams(dimension_semantics=sem, vmem_limit_bytes=VMEM_LIMIT)


def _layer_norm_rows(y, g, b):
    mu = jnp.mean(y, axis=-1, keepdims=True)
    d = y - mu
    var = jnp.mean(d * d, axis=-1, keepdims=True)
    return d * lax.rsqrt(var + LN_EPS) * g + b


def _proj_kernel(*refs, n_w, n_extra, n_out, epilogue):
    x_ref = refs[0]
    w_refs = refs[1:1 + n_w]
    extra = refs[1 + n_w:1 + n_w + n_extra]
    outs = refs[1 + n_w + n_extra:1 + n_w + n_extra + n_out]
    scratch = refs[1 + n_w + n_extra + n_out:]
    wb_refs = scratch[:n_w]
    rest = scratch[n_w:]

    @pl.when(pl.program_id(1) == 0)
    def _():
        for w_ref, wb_ref in zip(w_refs, wb_refs):
            wb_ref[...] = w_ref[...].astype(BF16)

    x = x_ref[...]
    accs = [jnp.dot(x, wb_ref[...], preferred_element_type=F32) for wb_ref in wb_refs]
    epilogue(accs, extra, outs, rest)


def _proj(x, weights, epilogue, out_specs, out_shapes, *, n_col_tiles, name, extra_in=(), extra_specs=(),
          extra_scratch=(), tn=TN):
    mp, k = x.shape
    n_w = len(weights)
    in_specs = [pl.BlockSpec((TM, k), lambda j, i: (i, 0))]
    args = [x]
    for w, layer, off in weights:
        in_specs.append(pl.BlockSpec((None, k, tn), functools.partial(
            lambda j, i, layer, off: (layer, 0, j + off), layer=layer, off=off)))
        args.append(w)
    in_specs += list(extra_specs)
    args += list(extra_in)
    kern = functools.partial(_proj_kernel, n_w=n_w, n_extra=len(extra_in), n_out=len(out_shapes),
                             epilogue=epilogue)
    return pl.pallas_call(
        kern,
        grid=(n_col_tiles, mp // TM),
        in_specs=in_specs,
        out_specs=out_specs,
        out_shape=out_shapes,
        scratch_shapes=[pltpu.VMEM((k, tn), BF16) for _ in range(n_w)] + list(extra_scratch),
        compiler_params=_cparams("arbitrary", "arbitrary"),
        name=name,
    )(*args)


def _tile_spec(tn=TN):
    return pl.BlockSpec((TM, tn), lambda j, i: (i, j))


def _ep_act(act, accs, extra, outs, rest):
    y = act(accs[0])
    for o in outs:
        o[...] = y.astype(o.dtype)


def _ep_swiglu(accs, extra, outs, rest):
    outs[0][...] = (jax.nn.silu(accs[0]) * accs[1]).astype(outs[0].dtype)


def _ep_conv(accs, extra, outs, rest, *, tiles_per_seq, n_prompt_tiles, dec_seq):
    cw_ref, p1_ref, p2_ref = extra
    y_ref, ztail_ref, zs_ref = outs
    (carry_ref,) = rest
    i = pl.program_id(1)
    z = accs[1] * accs[2]
    row = lax.broadcasted_iota(jnp.int32, z.shape, 0)
    r1 = pltpu.roll(z, 1, 0)
    r2 = pltpu.roll(z, 2, 0)
    w0 = cw_ref[0:1, :]
    w1 = cw_ref[1:2, :]
    w2 = cw_ref[2:3, :]

    @pl.when(i % tiles_per_seq == 0)
    def _():
        carry_ref[...] = jnp.zeros_like(carry_ref)

    @pl.when(i < n_prompt_tiles)
    def _():
        c6 = carry_ref[6:7, :]
        c7 = carry_ref[7:8, :]
        s1 = jnp.where(row == 0, c7, r1)
        s2 = jnp.where(row == 0, c6, jnp.where(row == 1, c7, r2))
        y = w0 * s2 + w1 * s1 + w2 * z
        y_ref[...] = (accs[0] * y).astype(y_ref.dtype)

    @pl.when(i >= n_prompt_tiles)
    def _():
        t = row % dec_seq
        s1 = jnp.where(t >= 1, r1, p1_ref[...])
        s2 = jnp.where(t >= 2, r2, p2_ref[...])
        y = w0 * s2 + w1 * s1 + w2 * z
        y_ref[...] = (accs[0] * y).astype(y_ref.dtype)
        zs_ref[...] = z

    carry_ref[...] = z[TM - 8:, :]
    ztail_ref[...] = z[TM - 8:, :]


def _down_ln_kernel(x_ref, w_ref, r_ref, g_ref, b_ref, o_ref, ob_ref, acc_ref, *, alpha):
    k = pl.program_id(1)

    @pl.when(k == 0)
    def _():
        acc_ref[...] = jnp.zeros_like(acc_ref)

    acc_ref[...] += jnp.dot(x_ref[...], w_ref[...].astype(BF16), preferred_element_type=F32)

    @pl.when(k == pl.num_programs(1) - 1)
    def _():
        y = _layer_norm_rows(alpha * r_ref[...] + acc_ref[...], g_ref[...], b_ref[...])
        o_ref[...] = y
        ob_ref[...] = y.astype(BF16)


def _down_ln(x, w, layer, resid, g, b, alpha):
    mp, k = x.shape
    d = resid.shape[1]
    return pl.pallas_call(
        functools.partial(_down_ln_kernel, alpha=alpha),
        grid=(mp // TM, k // TK),
        in_specs=[
            pl.BlockSpec((TM, TK), lambda i, kk: (i, kk)),
            pl.BlockSpec((None, TK, d), lambda i, kk: (layer, kk, 0)),
            pl.BlockSpec((TM, d), lambda i, kk: (i, 0)),
            pl.BlockSpec((1, d), lambda i, kk: (0, 0)),
            pl.BlockSpec((1, d), lambda i, kk: (0, 0)),
        ],
        out_specs=[pl.BlockSpec((TM, d), lambda i, kk: (i, 0)), pl.BlockSpec((TM, d), lambda i, kk: (i, 0))],
        out_shape=[jax.ShapeDtypeStruct((mp, d), F32), jax.ShapeDtypeStruct((mp, d), BF16)],
        scratch_shapes=[pltpu.VMEM((TM, d), F32)],
        compiler_params=_cparams("arbitrary", "arbitrary"),
        name="down_ln",
    )(x, w, resid, g, b)


def _spatial_kernel(u_ref, v_ref, g_ref, b_ref, wmix_ref, bias_ref, o_ref, vs_ref, vln_ref, *, n_prompt_tiles):
    i = pl.program_id(0)
    vln = _layer_norm_rows(v_ref[...], g_ref[...], b_ref[...])
    vln_ref[...] = vln.astype(BF16)

    @pl.when(i >= n_prompt_tiles)
    def _():
        vs_ref[...] = vln

    g_a = u_ref.shape[1] // H_A

    def chunk_body(c, carry):
        r0 = pl.multiple_of(c * CHUNK, CHUNK)
        for h in range(H_A):
            cols = slice(h * g_a, (h + 1) * g_a)
            s = jnp.dot(wmix_ref[h], vln_ref[pl.ds(r0, CHUNK), cols], preferred_element_type=F32)
            s = s + bias_ref[:, cols]
            o_ref[pl.ds(r0, CHUNK), cols] = (u_ref[pl.ds(r0, CHUNK), cols].astype(F32) * s).astype(o_ref.dtype)
        return carry

    lax.fori_loop(0, TM // CHUNK, chunk_body, 0)


def _spatial(u, v, ln_g, ln_b, wmix, bias, n_prompt_tiles):
    mp, e = u.shape
    sel = lambda i: jnp.where(i < n_prompt_tiles, 0, 1)
    return pl.pallas_call(
        functools.partial(_spatial_kernel, n_prompt_tiles=n_prompt_tiles),
        grid=(mp // TM,),
        in_specs=[
            pl.BlockSpec((TM, e), lambda i: (i, 0)),
            pl.BlockSpec((TM, e), lambda i: (i, 0)),
            pl.BlockSpec((1, e), lambda i: (0, 0)),
            pl.BlockSpec((1, e), lambda i: (0, 0)),
            pl.BlockSpec((None, H_A, CHUNK, CHUNK), lambda i: (sel(i), 0, 0, 0)),
            pl.BlockSpec((None, CHUNK, e), lambda i: (sel(i), 0, 0)),
        ],
        out_specs=[pl.BlockSpec((TM, e), lambda i: (i, 0)), pl.BlockSpec((TM, e), lambda i: (0, 0))],
        out_shape=[jax.ShapeDtypeStruct((mp, e), BF16), jax.ShapeDtypeStruct((TM, e), F32)],
        scratch_shapes=[pltpu.VMEM((TM, e), BF16)],
        compiler_params=_cparams("arbitrary"),
        name="spatial_mix",
    )(u, v, ln_g, ln_b, wmix, bias)


def _split3(x):
    hi = x.astype(BF16)
    r1 = x - hi.astype(F32)
    mid = r1.astype(BF16)
    lo = (r1 - mid.astype(F32)).astype(BF16)
    return hi, mid, lo


def _tri_matmul(tri, x):
    hi, mid, lo = _split3(x)
    out = jnp.dot(tri, lo, preferred_element_type=F32)
    out = out + jnp.dot(tri, mid, preferred_element_type=F32)
    return out + jnp.dot(tri, hi, preferred_element_type=F32)


def _logf_kernel(x_ref, w_ref, bf_ref, tri_ref, lf_ref, c_ref, carry_ref, *, tiles_per_seq):
    i = pl.program_id(0)

    @pl.when(i % tiles_per_seq == 0)
    def _():
        carry_ref[...] = jnp.zeros_like(carry_ref)

    z = jnp.dot(x_ref[...], w_ref[...].astype(BF16), preferred_element_type=F32) + bf_ref[...]
    lf = jax.nn.log_sigmoid(z)
    lf_ref[...] = lf
    c = _tri_matmul(tri_ref[...], lf) + carry_ref[0:1, :]
    c_ref[...] = c
    carry_ref[0:1, :] = c[TM - 1:TM, :]


def _logf(xb, w_f, b_f, tri, n_prompt_tiles, tiles_per_seq):
    mp, d = xb.shape
    h = w_f.shape[1]
    sel = lambda i: jnp.where(i < n_prompt_tiles, 0, 1)
    return pl.pallas_call(
        functools.partial(_logf_kernel, tiles_per_seq=tiles_per_seq),
        grid=(mp // TM,),
        in_specs=[
            pl.BlockSpec((TM, d), lambda i: (i, 0)),
            pl.BlockSpec((d, h), lambda i: (0, 0)),
            pl.BlockSpec((1, h), lambda i: (0, 0)),
            pl.BlockSpec((None, TM, TM), lambda i: (sel(i), 0, 0)),
        ],
        out_specs=[pl.BlockSpec((TM, h), lambda i: (i, 0)), pl.BlockSpec((TM, h), lambda i: (i, 0))],
        out_shape=[jax.ShapeDtypeStruct((mp, h), F32), jax.ShapeDtypeStruct((mp, h), F32)],
        scratch_shapes=[pltpu.VMEM((8, h), F32)],
        compiler_params=_cparams("arbitrary"),
        name="logf_cumsum",
    )(xb, w_f, b_f, tri)


def _fox_prompt_kernel(q_ref, k_ref, v_ref, cq_ref, ck_ref, o_ref, *, scale):
    qi = pl.program_id(2)
    q = q_ref[...]
    cq = cq_ref[...]
    tq, dh = q.shape

    def step(kv0, carry, masked):
        m, l, acc = carry
        k = k_ref[pl.ds(kv0, TKV), :]
        v = v_ref[pl.ds(kv0, TKV), :]
        s = lax.dot_general(q, k, (((1,), (1,)), ((), ())), preferred_element_type=F32)
        s = s * scale + cq - ck_ref[:, pl.ds(kv0, TKV)]
        if masked:
            row = lax.broadcasted_iota(jnp.int32, s.shape, 0)
            col = lax.broadcasted_iota(jnp.int32, s.shape, 1)
            s = jnp.where(row >= col, s, NEG_INF)
        m_new = jnp.maximum(m, jnp.max(s, axis=-1, keepdims=True))
        a = jnp.exp(m - m_new)
        p = jnp.exp(s - m_new)
        l = a * l + jnp.sum(p, axis=-1, keepdims=True)
        acc = a * acc + jnp.dot(p.astype(BF16), v, preferred_element_type=F32)
        return m_new, l, acc

    init = (jnp.full((tq, 1), NEG_INF, F32), jnp.zeros((tq, 1), F32), jnp.zeros((tq, dh), F32))
    carry = lax.fori_loop(
        0, qi * (tq // TKV),
        lambda j, c: step(pl.multiple_of(j * TKV, TKV), c, False), init)
    m, l, acc = step(pl.multiple_of(qi * tq, tq), carry, True)
    o_ref[...] = (acc / l).astype(o_ref.dtype)


def _fox_prompt(q, k, v, cq, ck, batch, seq, scale):
    dh = q.shape[1] // H_B
    nq = seq // TM
    return pl.pallas_call(
        functools.partial(_fox_prompt_kernel, scale=scale),
        grid=(batch, H_B, nq),
        in_specs=[
            pl.BlockSpec((TM, dh), lambda b, h, i: (b * nq + i, h)),
            pl.BlockSpec((seq, dh), lambda b, h, i: (b, h)),
            pl.BlockSpec((seq, dh), lambda b, h, i: (b, h)),
            pl.BlockSpec((None, TM, 1), lambda b, h, i: (h, b * nq + i, 0)),
            pl.BlockSpec((None, 1, seq), lambda b, h, i: (h, 0, b)),
        ],
        out_specs=pl.BlockSpec((TM, dh), lambda b, h, i: (b * nq + i, h)),
        out_shape=jax.ShapeDtypeStruct((batch * seq, H_B * dh), BF16),
        compiler_params=_cparams("arbitrary", "arbitrary", "arbitrary"),
        name="fox_prompt",
    )(q, k, v, cq, ck)


def _suffix_kernel(pt_ref, lf_ref, tri_ref, o_ref, carry_ref):
    @pl.when(pl.program_id(1) == 0)
    def _():
        carry_ref[...] = jnp.zeros_like(carry_ref)

    lf = lf_ref[...]
    within = _tri_matmul(tri_ref[...], lf)
    o_ref[...] = within + carry_ref[0:1, :]
    hi, mid, lo = _split3(lf)
    ones = jnp.ones((8, lf.shape[0]), BF16)
    tot = (jnp.dot(ones, lo, preferred_element_type=F32) + jnp.dot(ones, mid, preferred_element_type=F32)
           + jnp.dot(ones, hi, preferred_element_type=F32))
    carry_ref[...] = carry_ref[...] + tot


def _suffix(page_table, cache_logf2, layer, n_pool, tri_upper):
    n, n_pages = page_table.shape
    h = cache_logf2.shape[-1]
    grid_spec = pltpu.PrefetchScalarGridSpec(
        num_scalar_prefetch=1,
        grid=(n, n_pages),
        in_specs=[
            pl.BlockSpec((None, PAGE_SIZE, h),
                         lambda b, p, pt: (layer * n_pool + pt[b, n_pages - 1 - p], 0, 0)),
            pl.BlockSpec((PAGE_SIZE, PAGE_SIZE), lambda b, p, pt: (0, 0)),
        ],
        out_specs=pl.BlockSpec((None, None, PAGE_SIZE, h), lambda b, p, pt: (b, n_pages - 1 - p, 0, 0)),
        scratch_shapes=[pltpu.VMEM((8, h), F32)],
    )
    return pl.pallas_call(
        _suffix_kernel,
        grid_spec=grid_spec,
        out_shape=jax.ShapeDtypeStruct((n, n_pages, PAGE_SIZE, h), F32),
        compiler_params=_cparams("arbitrary", "arbitrary"),
        name="logf_suffix",
    )(page_table, cache_logf2, tri_upper)


def _fox_decode_kernel(pt_ref, q_ref, cq_ref, knew_ref, vnew_ref, cknew_ref, ckpast_ref, *rest, scale, dec_seq):
    kp_refs = rest[:PAGES_PER_STEP]
    vp_refs = rest[PAGES_PER_STEP:2 * PAGES_PER_STEP]
    o_ref = rest[2 * PAGES_PER_STEP]
    m_ref, l_ref, acc_ref = rest[2 * PAGES_PER_STEP + 1:]
    n = pl.program_id(0)
    pg = pl.program_id(1)
    dh = q_ref.shape[1] // H_B
    q = q_ref[...]

    def attend(h, k_h, v_h, ck_row, mask):
        hs = slice(h * dh, (h + 1) * dh)
        s = lax.dot_general(q[:, hs], k_h, (((1,), (1,)), ((), ())), preferred_element_type=F32)
        s = s * scale + cq_ref[h] - ck_row
        if mask is not None:
            s = jnp.where(mask, s, NEG_INF)
        m = m_ref[h]
        m_new = jnp.maximum(m, jnp.max(s, axis=-1, keepdims=True))
        a = jnp.exp(m - m_new)
        p = jnp.exp(s - m_new)
        l_ref[h] = a * l_ref[h] + jnp.sum(p, axis=-1, keepdims=True)
        acc_ref[h] = a * acc_ref[h] + jnp.dot(p.astype(BF16), v_h, preferred_element_type=F32)
        m_ref[h] = m_new

    @pl.when(pg == 0)
    def _():
        m_ref[...] = jnp.full(m_ref.shape, NEG_INF, F32)
        l_ref[...] = jnp.zeros_like(l_ref)
        acc_ref[...] = jnp.zeros_like(acc_ref)
        nk = knew_ref.shape[0]
        t = lax.broadcasted_iota(jnp.int32, (Q_ROWS, nk), 0)
        key = lax.broadcasted_iota(jnp.int32, (Q_ROWS, nk), 1)
        mask = (key >= n * dec_seq) & (key <= n * dec_seq + t)
        for h in range(H_B):
            hs = slice(h * dh, (h + 1) * dh)
            attend(h, knew_ref[:, hs].astype(BF16), vnew_ref[:, hs].astype(BF16), cknew_ref[h:h + 1, :], mask)

    for r in range(PAGES_PER_STEP):
        for h in range(H_B):
            hs = slice(h * dh, (h + 1) * dh)
            attend(h, kp_refs[r][:, hs].astype(BF16), vp_refs[r][:, hs].astype(BF16),
                   ckpast_ref[h:h + 1, r * PAGE_SIZE:(r + 1) * PAGE_SIZE], None)

    @pl.when(pg == pl.num_programs(1) - 1)
    def _():
        for h in range(H_B):
            o_ref[:, h * dh:(h + 1) * dh] = acc_ref[h] / l_ref[h]


def _fox_decode(page_table, q_s, cq_s, k_new, v_new, ck_new, ck_past, cache_k2, cache_v2, layer, n_pool, scale,
                dec_seq):
    n, n_pages = page_table.shape
    d = q_s.shape[-1]
    dh = d // H_B
    nk = k_new.shape[0]
    steps = n_pages // PAGES_PER_STEP

    def page_spec(r):
        return pl.BlockSpec((None, PAGE_SIZE, d),
                            lambda b, p, pt: (layer * n_pool + pt[b, p * PAGES_PER_STEP + r], 0, 0))

    in_specs = [
        pl.BlockSpec((None, Q_ROWS, d), lambda b, p, pt: (b, 0, 0)),
        pl.BlockSpec((H_B, Q_ROWS, 1), lambda b, p, pt: (0, b, 0)),
        pl.BlockSpec((nk, d), lambda b, p, pt: (0, 0)),
        pl.BlockSpec((nk, d), lambda b, p, pt: (0, 0)),
        pl.BlockSpec((H_B, nk), lambda b, p, pt: (0, 0)),
        pl.BlockSpec((None, H_B, PAGES_PER_STEP * PAGE_SIZE), lambda b, p, pt: (b, 0, p)),
    ]
    in_specs += [page_spec(r) for r in range(PAGES_PER_STEP)]
    in_specs += [page_spec(r) for r in range(PAGES_PER_STEP)]
    grid_spec = pltpu.PrefetchScalarGridSpec(
        num_scalar_prefetch=1,
        grid=(n, steps),
        in_specs=in_specs,
        out_specs=pl.BlockSpec((None, Q_ROWS, d), lambda b, p, pt: (b, 0, 0)),
        scratch_shapes=[pltpu.VMEM((H_B, Q_ROWS, 1), F32), pltpu.VMEM((H_B, Q_ROWS, 1), F32),
                        pltpu.VMEM((H_B, Q_ROWS, dh), F32)],
    )
    return pl.pallas_call(
        functools.partial(_fox_decode_kernel, scale=scale, dec_seq=dec_seq),
        grid_spec=grid_spec,
        out_shape=jax.ShapeDtypeStruct((n, Q_ROWS, d), F32),
        compiler_params=_cparams("arbitrary", "arbitrary"),
        name="fox_decode",
    )(page_table, q_s, cq_s, k_new, v_new, ck_new, ck_past,
      *([cache_k2] * PAGES_PER_STEP), *([cache_v2] * PAGES_PER_STEP))


def _router_kernel(x_ref, w_ref, b_ref, comb_ref, mask_ref):
    logits = jnp.dot(x_ref[...], w_ref[...].astype(BF16), preferred_element_type=F32) + b_ref[...]
    ne = logits.shape[1]
    col = lax.broadcasted_iota(jnp.int32, logits.shape, 1)
    m1 = jnp.max(logits, axis=-1, keepdims=True)
    i1 = jnp.min(jnp.where(logits == m1, col, ne), axis=-1, keepdims=True)
    rest = jnp.where(col == i1, -jnp.inf, logits)
    m2 = jnp.max(rest, axis=-1, keepdims=True)
    i2 = jnp.min(jnp.where(rest == m2, col, ne), axis=-1, keepdims=True)
    e2 = jnp.exp(m2 - m1)
    den = 1.0 + e2
    comb_ref[...] = jnp.where(col == i1, 1.0 / den, 0.0) + jnp.where(col == i2, e2 / den, 0.0)
    mask_ref[...] = ((col == i1) | (col == i2)).astype(jnp.int32)


def _router(xb, w_r, b_r, layer):
    mp, d = xb.shape
    ne = w_r.shape[-1]
    return pl.pallas_call(
        _router_kernel,
        grid=(mp // TM,),
        in_specs=[
            pl.BlockSpec((TM, d), lambda i: (i, 0)),
            pl.BlockSpec((None, d, ne), lambda i: (layer, 0, 0)),
            pl.BlockSpec((1, ne), lambda i: (0, 0)),
        ],
        out_specs=[pl.BlockSpec((TM, ne), lambda i: (i, 0)), pl.BlockSpec((TM, ne), lambda i: (i, 0))],
        out_shape=[jax.ShapeDtypeStruct((mp, ne), F32), jax.ShapeDtypeStruct((mp, ne), jnp.int32)],
        compiler_params=_cparams("arbitrary"),
        name="router",
    )(xb, w_r, b_r)


def _gmm_up_kernel(te_ref, tv_ref, tb_ref, x_ref, wg_ref, wu_ref, o_ref, wgb_ref, wub_ref):
    i = pl.program_id(1)
    prev = jnp.maximum(i - 1, 0)
    fresh = (i == 0) | (te_ref[i] != te_ref[prev])

    @pl.when(fresh)
    def _():
        wgb_ref[...] = wg_ref[...].astype(BF16)
        wub_ref[...] = wu_ref[...].astype(BF16)

    @pl.when(tv_ref[i] > 0)
    def _():
        x = x_ref[...]
        g = jnp.dot(x, wgb_ref[...], preferred_element_type=F32)
        u = jnp.dot(x, wub_ref[...], preferred_element_type=F32)
        o_ref[...] = (jax.nn.silu(g) * u).astype(o_ref.dtype)


def _gmm_up(te, tv, tb, xs, w_gate, w_up, layer):
    p, d = xs.shape
    f = w_gate.shape[-1]
    grid_spec = pltpu.PrefetchScalarGridSpec(
        num_scalar_prefetch=3,
        grid=(f // TN, p // TM),
        in_specs=[
            pl.BlockSpec((TM, d), lambda j, i, te, tv, tb: (tb[i], 0)),
            pl.BlockSpec((None, None, d, TN), lambda j, i, te, tv, tb: (layer, te[i], 0, j)),
            pl.BlockSpec((None, None, d, TN), lambda j, i, te, tv, tb: (layer, te[i], 0, j)),
        ],
        out_specs=pl.BlockSpec((TM, TN), lambda j, i, te, tv, tb: (tb[i], j)),
        scratch_shapes=[pltpu.VMEM((d, TN), BF16), pltpu.VMEM((d, TN), BF16)],
    )
    return pl.pallas_call(
        _gmm_up_kernel,
        grid_spec=grid_spec,
        out_shape=jax.ShapeDtypeStruct((p, f), BF16),
        compiler_params=_cparams("arbitrary", "arbitrary"),
        name="gmm_up",
    )(te, tv, tb, xs, w_gate, w_up)


TN_DOWN = 256


def _gmm_down_kernel(te_ref, tv_ref, tb_ref, h_ref, w_ref, gate_ref, o_ref, wb_ref):
    i = pl.program_id(1)
    prev = jnp.maximum(i - 1, 0)
    fresh = (i == 0) | (te_ref[i] != te_ref[prev])

    @pl.when(fresh)
    def _():
        wb_ref[...] = w_ref[...].astype(BF16)

    @pl.when(tv_ref[i] > 0)
    def _():
        y = jnp.dot(h_ref[...], wb_ref[...], preferred_element_type=F32)
        o_ref[...] = gate_ref[...] * y


def _gmm_down(te, tv, tb, h, w_down, gates, layer):
    p, f = h.shape
    d = w_down.shape[-1]
    grid_spec = pltpu.PrefetchScalarGridSpec(
        num_scalar_prefetch=3,
        grid=(d // TN_DOWN, p // TM),
        in_specs=[
            pl.BlockSpec((TM, f), lambda j, i, te, tv, tb: (tb[i], 0)),
            pl.BlockSpec((None, None, f, TN_DOWN), lambda j, i, te, tv, tb: (layer, te[i], 0, j)),
            pl.BlockSpec((TM, 1), lambda j, i, te, tv, tb: (tb[i], 0)),
        ],
        out_specs=pl.BlockSpec((TM, TN_DOWN), lambda j, i, te, tv, tb: (tb[i], j)),
        scratch_shapes=[pltpu.VMEM((f, TN_DOWN), BF16)],
    )
    return pl.pallas_call(
        _gmm_down_kernel,
        grid_spec=grid_spec,
        out_shape=jax.ShapeDtypeStruct((p, d), F32),
        compiler_params=_cparams("arbitrary", "arbitrary"),
        name="gmm_down",
    )(te, tv, tb, h, w_down, gates)


def _combine_ln_kernel(r_ref, ya_ref, yb_ref, g_ref, b_ref, o_ref, ob_ref, *, alpha):
    y = _layer_norm_rows(alpha * r_ref[...] + (ya_ref[...] + yb_ref[...]), g_ref[...], b_ref[...])
    o_ref[...] = y
    ob_ref[...] = y.astype(BF16)


def _combine_ln(resid, ya, yb, g, b, alpha):
    mp, d = resid.shape
    row = pl.BlockSpec((TM, d), lambda i: (i, 0))
    vec = pl.BlockSpec((1, d), lambda i: (0, 0))
    return pl.pallas_call(
        functools.partial(_combine_ln_kernel, alpha=alpha),
        grid=(mp // TM,),
        in_specs=[row, row, row, vec, vec],
        out_specs=[row, row],
        out_shape=[jax.ShapeDtypeStruct((mp, d), F32), jax.ShapeDtypeStruct((mp, d), BF16)],
        compiler_params=_cparams("arbitrary"),
        name="combine_ln",
    )(resid, ya, yb, g, b)


def _moe(x, xb, m_real, w_router, b_router, w_gate, w_up, w_down, layer, ln_g, ln_b, alpha):
    mp, d = x.shape
    ne = w_router.shape[-1]
    comb, mask = _router(xb, w_router, b_router, layer)
    mask = mask * (jnp.arange(mp, dtype=jnp.int32) < m_real)[:, None].astype(jnp.int32)
    rank = jnp.cumsum(mask, axis=0) - mask
    counts = jnp.sum(mask, axis=0)
    tiles_e = (counts + TM - 1) // TM
    tile_end = jnp.cumsum(tiles_e)
    tile_start = tile_end - tiles_e
    n_tiles = (TOP_K * m_real) // TM + ne
    p = n_tiles * TM
    pos = tile_start[None, :] * TM + rank
    flat = jnp.where(mask > 0, pos, p).reshape(-1)
    tok = jnp.broadcast_to(jnp.arange(mp, dtype=jnp.int32)[:, None], (mp, ne)).reshape(-1)
    tok_of = jnp.zeros((p,), jnp.int32).at[flat].set(tok, mode="drop")
    gate_of = jnp.zeros((p,), F32).at[flat].set(comb.reshape(-1), mode="drop")
    pos_a = jnp.max(jnp.where(mask > 0, pos, 0), axis=1)
    pos_b = jnp.min(jnp.where(mask > 0, pos, p), axis=1)
    pos_b = jnp.where(pos_b >= p, 0, pos_b)
    tile_id = jnp.arange(n_tiles, dtype=jnp.int32)
    n_active = tile_end[-1]
    last = jnp.maximum(n_active - 1, 0)
    tile_c = jnp.minimum(tile_id, last)
    te = jnp.searchsorted(tile_end, tile_c, side="right").astype(jnp.int32)
    te = jnp.minimum(te, ne - 1)
    tv = jnp.clip(counts[te] - (tile_c - tile_start[te]) * TM, 0, TM)
    tv = jnp.where(tile_id < n_active, tv, 0).astype(jnp.int32)
    tb = tile_c.astype(jnp.int32)

    xs = jnp.take(xb, tok_of, axis=0)
    h = _gmm_up(te, tv, tb, xs, w_gate, w_up, layer)
    y = _gmm_down(te, tv, tb, h, w_down, gate_of[:, None], layer)
    ya = jnp.take(y, pos_a, axis=0)
    yb = jnp.take(y, pos_b, axis=0)
    return _combine_ln(x, ya, yb, ln_g, ln_b, alpha)


def kernel(x_prompt, x_sample, cache_k, cache_v, cache_logf, state_conv, page_table, a_w_in, a_ln_g, a_ln_b, a_w_s, a_b_s, a_w_out, b_w_in, b_b_f, b_w_o, c_w_in, c_conv_w, c_w_out, ffn_w_gate, ffn_w_up, ffn_w_down, moe_w_router, moe_b_router, moe_w_gate, moe_w_up, moe_w_down, ln1_g, ln1_b, ln2_g, ln2_b):
    batch, seq, d = x_prompt.shape
    dec_batch, dec_seq, _ = x_sample.shape
    depth = ln1_g.shape[0]
    alpha = (2.0 * depth) ** 0.25
    m_prompt = batch * seq
    m_samp = dec_batch * dec_seq
    m_real = m_prompt + m_samp
    assert seq % TM == 0 and m_samp <= CHUNK and TM % CHUNK == 0 and dec_seq <= 8
    n_prompt_tiles = m_prompt // TM
    tiles_per_seq = seq // TM
    mp = m_prompt + TM
    n_pool = cache_k.shape[1]
    dh = d // H_B
    e_a = a_w_out.shape[1]
    e_c = c_w_out.shape[1]
    g_a = e_a // H_A
    assert e_a == d and e_c == d and g_a % 128 == 0 and dh % 128 == 0

    x = jnp.concatenate([x_prompt.reshape(m_prompt, d), x_sample.reshape(m_samp, d),
                         jnp.zeros((mp - m_real, d), F32)], axis=0)
    xb = x.astype(BF16)

    r = jnp.arange(TM)
    tri_prompt = r[:, None] >= r[None, :]
    tri_tail = tri_prompt & ((r[:, None] // dec_seq) == (r[None, :] // dec_seq))
    tri = jnp.stack([tri_prompt, tri_tail]).astype(BF16)
    rp = jnp.arange(PAGE_SIZE)
    tri_upper = (rp[None, :] > rp[:, None]).astype(BF16)

    k_p, v_p, lf_p, k_s, v_s, lf_s, conv_p, conv_s, chunk_v_s = [], [], [], [], [], [], [], [], []
    n_mix = 3
    for li in range(depth):
        kind = li % n_mix
        j = li // n_mix
        if kind == 0:
            nt = e_a // TN
            (u,) = _proj(xb, [(a_w_in, j, 0)], functools.partial(_ep_act, jax.nn.gelu), [_tile_spec()],
                         [jax.ShapeDtypeStruct((mp, e_a), BF16)], n_col_tiles=nt, name="a_in_u")
            (v,) = _proj(xb, [(a_w_in, j, nt)], functools.partial(_ep_act, jax.nn.gelu), [_tile_spec()],
                         [jax.ShapeDtypeStruct((mp, e_a), F32)], n_col_tiles=nt, name="a_in_v")
            rc = jnp.arange(CHUNK)
            causal = rc[:, None] >= rc[None, :]
            w_prompt = jnp.where(causal[None], a_w_s[j], 0.0)
            same = (rc[:, None] // dec_seq) == (rc[None, :] // dec_seq)
            t_in = rc % dec_seq
            w_tail = jnp.where((causal & same)[None], a_w_s[j][:, t_in][:, :, t_in], 0.0)
            wmix = jnp.stack([w_prompt, w_tail]).astype(BF16)
            bias_p = jnp.repeat(a_b_s[j].T, g_a, axis=1)
            bias_t = jnp.repeat(a_b_s[j][:, t_in].T, g_a, axis=1)
            bias = jnp.stack([bias_p, bias_t])
            us, vs = _spatial(u, v, a_ln_g[j][None], a_ln_b[j][None], wmix, bias, n_prompt_tiles)
            chunk_v_s.append(vs[:m_samp].reshape(dec_batch, dec_seq, e_a))
            x, xb = _down_ln(us, a_w_out, j, x, ln1_g[li][None], ln1_b[li][None], alpha)
        elif kind == 1:
            nt = d // TN
            (q16,) = _proj(xb, [(b_w_in, j, 0)], functools.partial(_ep_act, lambda a: a), [_tile_spec()],
                           [jax.ShapeDtypeStruct((mp, d), BF16)], n_col_tiles=nt, name="b_in_q")
            k32, k16 = _proj(xb, [(b_w_in, j, nt)], functools.partial(_ep_act, lambda a: a),
                             [_tile_spec(), _tile_spec()],
                             [jax.ShapeDtypeStruct((mp, d), F32), jax.ShapeDtypeStruct((mp, d), BF16)],
                             n_col_tiles=nt, name="b_in_k")
            v32, v16 = _proj(xb, [(b_w_in, j, 2 * nt)], functools.partial(_ep_act, lambda a: a),
                             [_tile_spec(), _tile_spec()],
                             [jax.ShapeDtypeStruct((mp, d), F32), jax.ShapeDtypeStruct((mp, d), BF16)],
                             n_col_tiles=nt, name="b_in_v")
            w_f = b_w_in[j][:, 3 * d:]
            lf, c = _logf(xb, w_f, b_b_f[j][None, :], tri, n_prompt_tiles, tiles_per_seq)
            scale = dh ** -0.5
            c_t = c.T
            cq = c_t[:, :m_prompt, None]
            ck = c_t[:, None, :m_prompt]
            o_prompt = _fox_prompt(q16, k16, v16, cq, ck, batch, seq, scale)

            cache_k2 = cache_k.reshape(-1, PAGE_SIZE, d)
            cache_v2 = cache_v.reshape(-1, PAGE_SIZE, d)
            cache_lf2 = cache_logf.reshape(-1, PAGE_SIZE, H_B)
            suffix = _suffix(page_table, cache_lf2, j, n_pool, tri_upper)
            n_pages = page_table.shape[1]
            ck_past = -suffix.reshape(dec_batch, n_pages * PAGE_SIZE, H_B).transpose(0, 2, 1)
            qpad = ((0, 0), (0, Q_ROWS - dec_seq), (0, 0))
            q_s = jnp.pad(q16[m_prompt:m_real].reshape(dec_batch, dec_seq, d), qpad)
            c_s = jnp.pad(c[m_prompt:m_real].reshape(dec_batch, dec_seq, H_B), qpad)
            cq_s = c_s.reshape(dec_batch * Q_ROWS, H_B).T[:, :, None]
            o_samp = _fox_decode(page_table, q_s, cq_s, k32[m_prompt:m_prompt + CHUNK],
                                 v32[m_prompt:m_prompt + CHUNK], c_t[:, m_prompt:m_prompt + CHUNK], ck_past,
                                 cache_k2, cache_v2, j, n_pool, scale, dec_seq)
            o_samp = o_samp[:, :dec_seq].reshape(m_samp, d).astype(BF16)
            o_all = jnp.concatenate([o_prompt, o_samp, jnp.zeros((mp - m_real, d), BF16)], axis=0)
            x, xb = _down_ln(o_all, b_w_o, j, x, ln1_g[li][None], ln1_b[li][None], alpha)
            k_p.append(k32[:m_prompt].reshape(batch, seq, H_B, dh))
            v_p.append(v32[:m_prompt].reshape(batch, seq, H_B, dh))
            lf_p.append(lf[:m_prompt].reshape(batch, seq, H_B))
            k_s.append(k32[m_prompt:m_real].reshape(dec_batch, dec_seq, H_B, dh))
            v_s.append(v32[m_prompt:m_real].reshape(dec_batch, dec_seq, H_B, dh))
            lf_s.append(lf[m_prompt:m_real].reshape(dec_batch, dec_seq, H_B))
        else:
            nt = e_c // TN
            st = state_conv[j]
            p1 = jnp.zeros((dec_batch, dec_seq, e_c), F32).at[:, 0].set(st[:, 1])
            p2 = jnp.zeros((dec_batch, dec_seq, e_c), F32).at[:, 0].set(st[:, 0]).at[:, 1].set(st[:, 1])
            pad = jnp.zeros((TM - m_samp, e_c), F32)
            p1 = jnp.concatenate([p1.reshape(m_samp, e_c), pad], axis=0)
            p2 = jnp.concatenate([p2.reshape(m_samp, e_c), pad], axis=0)
            n_tiles = mp // TM
            ep = functools.partial(_ep_conv, tiles_per_seq=tiles_per_seq, n_prompt_tiles=n_prompt_tiles,
                                   dec_seq=dec_seq)
            bgy, ztail, zs = _proj(
                xb, [(c_w_in, j, 0), (c_w_in, j, nt), (c_w_in, j, 2 * nt)], ep,
                [_tile_spec(), pl.BlockSpec((8, TN), lambda jj, i: (i, jj)),
                 pl.BlockSpec((TM, TN), lambda jj, i: (0, jj))],
                [jax.ShapeDtypeStruct((mp, e_c), BF16), jax.ShapeDtypeStruct((n_tiles * 8, e_c), F32),
                 jax.ShapeDtypeStruct((TM, e_c), F32)],
                n_col_tiles=nt, name="c_in_conv",
                extra_in=[c_conv_w, p1, p2],
                extra_specs=[pl.BlockSpec((None, CONV_W, TN), lambda jj, i: (j, 0, jj)),
                             pl.BlockSpec((TM, TN), lambda jj, i: (0, jj)),
                             pl.BlockSpec((TM, TN), lambda jj, i: (0, jj))],
                extra_scratch=[pltpu.VMEM((8, TN), F32)])
            zt = ztail.reshape(n_tiles, 8, e_c)
            last_tiles = jnp.arange(batch) * tiles_per_seq + tiles_per_seq - 1
            conv_p.append(zt[last_tiles][:, 8 - (CONV_W - 1):, :])
            conv_s.append(zs[:m_samp].reshape(dec_batch, dec_seq, e_c)[:, dec_seq - (CONV_W - 1):, :])
            x, xb = _down_ln(bgy, c_w_out, j, x, ln1_g[li][None], ln1_b[li][None], alpha)

        f = li // 2
        if li % 2 == 0:
            nt = ffn_w_gate.shape[-1] // TN
            (h,) = _proj(xb, [(ffn_w_gate, f, 0), (ffn_w_up, f, 0)], _ep_swiglu, [_tile_spec()],
                         [jax.ShapeDtypeStruct((mp, ffn_w_gate.shape[-1]), BF16)], n_col_tiles=nt, name="ffn_up")
            x, xb = _down_ln(h, ffn_w_down, f, x, ln2_g[li][None], ln2_b[li][None], alpha)
        else:
            x, xb = _moe(x, xb, m_real, moe_w_router, moe_b_router[f][None], moe_w_gate, moe_w_up, moe_w_down, f,
                         ln2_g[li][None], ln2_b[li][None], alpha)

    y_prompt = x[:m_prompt].reshape(batch, seq, d)
    y_sample = x[m_prompt:m_real].reshape(dec_batch, dec_seq, d)
    return (y_prompt, y_sample, jnp.stack(k_p), jnp.stack(v_p), jnp.stack(lf_p), jnp.stack(k_s),
            jnp.stack(v_s), jnp.stack(lf_s), jnp.stack(conv_p), jnp.stack(conv_s), jnp.stack(chunk_v_s))
```

```python
import functools

import jax
import jax.numpy as jnp
from jax import lax
from jax.experimental import pallas as pl
from jax.experimental.pallas import tpu as pltpu

F32 = jnp.float32
BF16 = jnp.bfloat16
U32 = jnp.uint32

H_A = 16
CHUNK = 128
H_B = 16
PAGE_SIZE = 128
CONV_W = 3
TOP_K = 2
LN_EPS = 1e-5
NEG_INF = -1e30

SUBLANES = 8
LANES = 128

TM = 512
TN = 512
TM_WIDE_K = 256
TKV = 512
HEADS_PER_STEP = 2
PAGES_PER_STEP = 4
SUFFIX_PAGES_PER_STEP = 8
Q_ROWS = 16
TF = 256
N_SUB = 4
TC = 256

VMEM_LIMIT = 56 * 1024 * 1024


def _cparams(*sem):
    return pltpu.CompilerParams(dimension_semantics=sem, vmem_limit_bytes=VMEM_LIMIT)


def _layer_norm_rows(y, g, b):
    mu = jnp.mean(y, axis=-1, keepdims=True)
    d = y - mu
    var = jnp.mean(d * d, axis=-1, keepdims=True)
    return d * lax.rsqrt(var + LN_EPS) * g + b


def _proj_kernel(*refs, n_w, n_extra, n_out, epilogue):
    x_ref = refs[0]
    w_refs = refs[1:1 + n_w]
    extra = refs[1 + n_w:1 + n_w + n_extra]
    outs = refs[1 + n_w + n_extra:1 + n_w + n_extra + n_out]
    scratch = refs[1 + n_w + n_extra + n_out:]
    wb_refs = scratch[:n_w]
    rest = scratch[n_w:]

    @pl.when(pl.program_id(1) == 0)
    def _():
        for w_ref, wb_ref in zip(w_refs, wb_refs):
            wb_ref[...] = w_ref[...].astype(BF16)

    x = x_ref[...]
    accs = [jnp.dot(x, wb_ref[...], preferred_element_type=F32) for wb_ref in wb_refs]
    epilogue(accs, extra, outs, rest)


def _proj(x, weights, epilogue, out_specs, out_shapes, *, n_col_tiles, name, extra_in=(), extra_specs=(),
          extra_scratch=(), tn=TN):
    mp, k = x.shape
    n_w = len(weights)
    in_specs = [pl.BlockSpec((TM, k), lambda j, i: (i, 0))]
    args = [x]
    for w, layer, off in weights:
        in_specs.append(pl.BlockSpec((None, k, tn), functools.partial(
            lambda j, i, layer, off: (layer, 0, j + off), layer=layer, off=off)))
        args.append(w)
    in_specs += list(extra_specs)
    args += list(extra_in)
    kern = functools.partial(_proj_kernel, n_w=n_w, n_extra=len(extra_in), n_out=len(out_shapes),
                             epilogue=epilogue)
    return pl.pallas_call(
        kern,
        grid=(n_col_tiles, mp // TM),
        in_specs=in_specs,
        out_specs=out_specs,
        out_shape=out_shapes,
        scratch_shapes=[pltpu.VMEM((k, tn), BF16) for _ in range(n_w)] + list(extra_scratch),
        compiler_params=_cparams("arbitrary", "arbitrary"),
        name=name,
    )(*args)


def _tile_spec(tn=TN):
    return pl.BlockSpec((TM, tn), lambda j, i: (i, j))


def _ep_act(act, accs, extra, outs, rest):
    y = act(accs[0])
    for o in outs:
        o[...] = y.astype(o.dtype)


def _ep_swiglu(accs, extra, outs, rest):
    outs[0][...] = (jax.nn.silu(accs[0]) * accs[1]).astype(outs[0].dtype)


def _ep_conv(accs, extra, outs, rest, *, tiles_per_seq, n_prompt_tiles, dec_seq):
    cw_ref, p1_ref, p2_ref = extra
    y_ref, ztail_ref, zs_ref = outs
    (carry_ref,) = rest
    i = pl.program_id(1)
    z = accs[1] * accs[2]
    row = lax.broadcasted_iota(jnp.int32, z.shape, 0)
    r1 = pltpu.roll(z, 1, 0)
    r2 = pltpu.roll(z, 2, 0)
    w0 = cw_ref[0:1, :]
    w1 = cw_ref[1:2, :]
    w2 = cw_ref[2:3, :]

    @pl.when(i % tiles_per_seq == 0)
    def _():
        carry_ref[...] = jnp.zeros_like(carry_ref)

    @pl.when(i < n_prompt_tiles)
    def _():
        c6 = carry_ref[6:7, :]
        c7 = carry_ref[7:8, :]
        s1 = jnp.where(row == 0, c7, r1)
        s2 = jnp.where(row == 0, c6, jnp.where(row == 1, c7, r2))
        y = w0 * s2 + w1 * s1 + w2 * z
        y_ref[...] = (accs[0] * y).astype(y_ref.dtype)

    @pl.when(i >= n_prompt_tiles)
    def _():
        t = row % dec_seq
        s1 = jnp.where(t >= 1, r1, p1_ref[...])
        s2 = jnp.where(t >= 2, r2, p2_ref[...])
        y = w0 * s2 + w1 * s1 + w2 * z
        y_ref[...] = (accs[0] * y).astype(y_ref.dtype)
        zs_ref[...] = z

    carry_ref[...] = z[TM - SUBLANES:, :]
    ztail_ref[...] = z[TM - SUBLANES:, :]


def _out_ln_kernel(x_ref, w_ref, r_ref, g_ref, b_ref, o_ref, ob_ref, *, alpha):
    acc = jnp.dot(x_ref[...], w_ref[...], preferred_element_type=F32)
    y = _layer_norm_rows(alpha * r_ref[...] + acc, g_ref[...], b_ref[...])
    o_ref[...] = y
    ob_ref[...] = y.astype(BF16)


def _out_ln(x, w, resid, g, b, alpha, tm):
    mp, k = x.shape
    d = resid.shape[1]
    row = lambda i: (i, 0)
    fixed = lambda i: (0, 0)
    return pl.pallas_call(
        functools.partial(_out_ln_kernel, alpha=alpha),
        grid=(mp // tm,),
        in_specs=[
            pl.BlockSpec((tm, k), row),
            pl.BlockSpec((k, d), fixed, pipeline_mode=pl.Buffered(1)),
            pl.BlockSpec((tm, d), row),
            pl.BlockSpec((1, d), fixed),
            pl.BlockSpec((1, d), fixed),
        ],
        out_specs=[pl.BlockSpec((tm, d), row), pl.BlockSpec((tm, d), row)],
        out_shape=[jax.ShapeDtypeStruct((mp, d), F32), jax.ShapeDtypeStruct((mp, d), BF16)],
        compiler_params=_cparams("arbitrary"),
        name="out_ln",
    )(x, w, resid, g, b)


def _spatial_kernel(u_ref, v_ref, g_ref, b_ref, wmix_ref, bias_ref, o_ref, vs_ref, vln_ref, *, n_prompt_tiles):
    i = pl.program_id(0)
    vln = _layer_norm_rows(v_ref[...], g_ref[...], b_ref[...])
    vln_ref[...] = vln.astype(BF16)

    @pl.when(i >= n_prompt_tiles)
    def _():
        vs_ref[...] = vln

    g_a = u_ref.shape[1] // H_A

    def chunk_body(c, carry):
        r0 = pl.multiple_of(c * CHUNK, CHUNK)
        for h in range(H_A):
            cols = slice(h * g_a, (h + 1) * g_a)
            s = jnp.dot(wmix_ref[h], vln_ref[pl.ds(r0, CHUNK), cols], preferred_element_type=F32)
            s = s + bias_ref[:, cols]
            o_ref[pl.ds(r0, CHUNK), cols] = (u_ref[pl.ds(r0, CHUNK), cols].astype(F32) * s).astype(o_ref.dtype)
        return carry

    lax.fori_loop(0, TM // CHUNK, chunk_body, 0)


def _spatial(u, v, ln_g, ln_b, wmix, bias, n_prompt_tiles):
    mp, e = u.shape
    sel = lambda i: jnp.where(i < n_prompt_tiles, 0, 1)
    return pl.pallas_call(
        functools.partial(_spatial_kernel, n_prompt_tiles=n_prompt_tiles),
        grid=(mp // TM,),
        in_specs=[
            pl.BlockSpec((TM, e), lambda i: (i, 0)),
            pl.BlockSpec((TM, e), lambda i: (i, 0)),
            pl.BlockSpec((1, e), lambda i: (0, 0)),
            pl.BlockSpec((1, e), lambda i: (0, 0)),
            pl.BlockSpec((None, H_A, CHUNK, CHUNK), lambda i: (sel(i), 0, 0, 0)),
            pl.BlockSpec((None, CHUNK, e), lambda i: (sel(i), 0, 0)),
        ],
        out_specs=[pl.BlockSpec((TM, e), lambda i: (i, 0)), pl.BlockSpec((TM, e), lambda i: (0, 0))],
        out_shape=[jax.ShapeDtypeStruct((mp, e), BF16), jax.ShapeDtypeStruct((TM, e), F32)],
        scratch_shapes=[pltpu.VMEM((TM, e), BF16)],
        compiler_params=_cparams("arbitrary"),
        name="spatial_mix",
    )(u, v, ln_g, ln_b, wmix, bias)


def _split3(x):
    hi = x.astype(BF16)
    r1 = x - hi.astype(F32)
    mid = r1.astype(BF16)
    lo = (r1 - mid.astype(F32)).astype(BF16)
    return hi, mid, lo


def _tri_matmul(tri, x):
    hi, mid, lo = _split3(x)
    out = jnp.dot(tri, lo, preferred_element_type=F32)
    out = out + jnp.dot(tri, mid, preferred_element_type=F32)
    return out + jnp.dot(tri, hi, preferred_element_type=F32)


def _logf_kernel(x_ref, w_ref, bf_ref, tri_ref, lf_ref, c_ref, carry_ref, *, tiles_per_seq):
    i = pl.program_id(0)

    @pl.when(i % tiles_per_seq == 0)
    def _():
        carry_ref[...] = jnp.zeros_like(carry_ref)

    z = jnp.dot(x_ref[...], w_ref[...].astype(BF16), preferred_element_type=F32) + bf_ref[...]
    lf = jax.nn.log_sigmoid(z)
    lf_ref[...] = lf
    c = _tri_matmul(tri_ref[...], lf) + carry_ref[0:1, :]
    c_ref[...] = c
    carry_ref[0:1, :] = c[TM - 1:TM, :]


def _logf(xb, w_f, b_f, tri, n_prompt_tiles, tiles_per_seq):
    mp, d = xb.shape
    h = w_f.shape[1]
    sel = lambda i: jnp.where(i < n_prompt_tiles, 0, 1)
    return pl.pallas_call(
        functools.partial(_logf_kernel, tiles_per_seq=tiles_per_seq),
        grid=(mp // TM,),
        in_specs=[
            pl.BlockSpec((TM, d), lambda i: (i, 0)),
            pl.BlockSpec((d, h), lambda i: (0, 0)),
            pl.BlockSpec((1, h), lambda i: (0, 0)),
            pl.BlockSpec((None, TM, TM), lambda i: (sel(i), 0, 0)),
        ],
        out_specs=[pl.BlockSpec((TM, h), lambda i: (i, 0)), pl.BlockSpec((TM, h), lambda i: (i, 0))],
        out_shape=[jax.ShapeDtypeStruct((mp, h), F32), jax.ShapeDtypeStruct((mp, h), F32)],
        scratch_shapes=[pltpu.VMEM((SUBLANES, h), F32)],
        compiler_params=_cparams("arbitrary"),
        name="logf_cumsum",
    )(xb, w_f, b_f, tri)


def _fox_prompt_kernel(q_ref, k_ref, v_ref, cq_ref, ck_ref, o_ref, *, scale):
    qi = pl.program_id(2)
    tq = q_ref.shape[0]
    dh = q_ref.shape[1] // HEADS_PER_STEP
    heads = range(HEADS_PER_STEP)
    qs = [q_ref[:, g * dh:(g + 1) * dh] for g in heads]
    cqs = [cq_ref[g] for g in heads]

    def step(g, kv0, carry, masked):
        m, l, acc = carry
        k = k_ref[pl.ds(kv0, TKV), g * dh:(g + 1) * dh]
        v = v_ref[pl.ds(kv0, TKV), g * dh:(g + 1) * dh]
        s = lax.dot_general(qs[g], k, (((1,), (1,)), ((), ())), preferred_element_type=F32)
        s = s * scale + cqs[g] - ck_ref[g, :, pl.ds(kv0, TKV)]
        if masked:
            row = lax.broadcasted_iota(jnp.int32, s.shape, 0)
            col = lax.broadcasted_iota(jnp.int32, s.shape, 1)
            s = jnp.where(row >= col, s, NEG_INF)
        m_new = jnp.maximum(m, jnp.max(s, axis=-1, keepdims=True))
        a = jnp.exp(m - m_new)
        p = jnp.exp(s - m_new)
        l = a * l + jnp.sum(p, axis=-1, keepdims=True)
        acc = a * acc + jnp.dot(p.astype(BF16), v, preferred_element_type=F32)
        return m_new, l, acc

    init = tuple((jnp.full((tq, 1), NEG_INF, F32), jnp.zeros((tq, 1), F32), jnp.zeros((tq, dh), F32))
                 for _ in heads)

    def body(j, carries):
        kv0 = pl.multiple_of(j * TKV, TKV)
        return tuple(step(g, kv0, carries[g], False) for g in heads)

    carries = lax.fori_loop(0, qi * (tq // TKV), body, init)
    for g in heads:
        m, l, acc = step(g, pl.multiple_of(qi * tq, tq), carries[g], True)
        o_ref[:, g * dh:(g + 1) * dh] = (acc / l).astype(o_ref.dtype)


def _fox_prompt(q, k, v, cq, ck, batch, seq, scale):
    dh = q.shape[1] // H_B
    nq = seq // TM
    w = HEADS_PER_STEP * dh
    return pl.pallas_call(
        functools.partial(_fox_prompt_kernel, scale=scale),
        grid=(batch, H_B // HEADS_PER_STEP, nq),
        in_specs=[
            pl.BlockSpec((TM, w), lambda b, h, i: (b * nq + i, h)),
            pl.BlockSpec((seq, w), lambda b, h, i: (b, h)),
            pl.BlockSpec((seq, w), lambda b, h, i: (b, h)),
            pl.BlockSpec((HEADS_PER_STEP, TM, 1), lambda b, h, i: (h, b * nq + i, 0)),
            pl.BlockSpec((HEADS_PER_STEP, 1, seq), lambda b, h, i: (h, 0, b)),
        ],
        out_specs=pl.BlockSpec((TM, w), lambda b, h, i: (b * nq + i, h)),
        out_shape=jax.ShapeDtypeStruct((batch * seq, H_B * dh), BF16),
        compiler_params=_cparams("arbitrary", "arbitrary", "arbitrary"),
        name="fox_prompt",
    )(q, k, v, cq, ck)


def _suffix_kernel(pt_ref, *refs):
    sp = SUFFIX_PAGES_PER_STEP
    lf_refs = refs[:sp]
    tri_ref, o_ref, carry_ref = refs[sp:]

    @pl.when(pl.program_id(1) == 0)
    def _():
        carry_ref[...] = jnp.zeros_like(carry_ref)

    carry = carry_ref[0:1, :]
    tri = tri_ref[...]
    for r in reversed(range(sp)):
        lf = lf_refs[r][...]
        o_ref[r] = _tri_matmul(tri, lf) + carry
        carry = carry + jnp.sum(lf, axis=0, keepdims=True)
    carry_ref[0:1, :] = carry


def _suffix(page_table, cache_lf_rows, layer, n_pool, tri_upper):
    n, n_pages = page_table.shape
    h = cache_lf_rows.shape[-1]
    sp = SUFFIX_PAGES_PER_STEP
    groups = n_pages // sp

    def page_spec(r):
        return pl.BlockSpec((PAGE_SIZE, h),
                            lambda b, p, pt: (layer * n_pool + pt[b, (groups - 1 - p) * sp + r], 0))

    grid_spec = pltpu.PrefetchScalarGridSpec(
        num_scalar_prefetch=1,
        grid=(n, groups),
        in_specs=[page_spec(r) for r in range(sp)] + [
            pl.BlockSpec((PAGE_SIZE, PAGE_SIZE), lambda b, p, pt: (0, 0))],
        out_specs=pl.BlockSpec((None, sp, PAGE_SIZE, h), lambda b, p, pt: (b, groups - 1 - p, 0, 0)),
        scratch_shapes=[pltpu.VMEM((SUBLANES, h), F32)],
    )
    return pl.pallas_call(
        _suffix_kernel,
        grid_spec=grid_spec,
        out_shape=jax.ShapeDtypeStruct((n, n_pages, PAGE_SIZE, h), F32),
        compiler_params=_cparams("arbitrary", "arbitrary"),
        name="logf_suffix",
    )(page_table, *([cache_lf_rows] * sp), tri_upper)


def _fox_decode_kernel(pt_ref, q_ref, cq_ref, knew_ref, vnew_ref, cknew_ref, ckpast_ref, *rest, scale, dec_seq):
    pp = PAGES_PER_STEP
    kp_refs = rest[:pp]
    vp_refs = rest[pp:2 * pp]
    o_ref = rest[2 * pp]
    s_ref, p_ref, m_ref, l_ref, acc_ref = rest[2 * pp + 1:]
    n = pl.program_id(0)
    pg = pl.program_id(1)
    dh = q_ref.shape[1]
    nt_dims = (((1,), (1,)), ((), ()))

    def hrows(h):
        return slice(h * Q_ROWS, (h + 1) * Q_ROWS)

    def softmax_update(width):
        s = s_ref[:, :width]
        m_old = m_ref[...]
        m_new = jnp.maximum(m_old, jnp.max(s, axis=-1, keepdims=True))
        a = jnp.exp(m_old - m_new)
        p = jnp.exp(s - m_new)
        l_ref[...] = a * l_ref[...] + jnp.sum(p, axis=-1, keepdims=True)
        m_ref[...] = m_new
        p_ref[:, :width] = p.astype(BF16)
        acc_ref[...] = a * acc_ref[...]

    @pl.when(pg == 0)
    def _():
        m_ref[...] = jnp.full(m_ref.shape, NEG_INF, F32)
        l_ref[...] = jnp.zeros_like(l_ref)
        acc_ref[...] = jnp.zeros_like(acc_ref)
        nk = knew_ref.shape[0]
        t = lax.broadcasted_iota(jnp.int32, (Q_ROWS, nk), 0)
        key = lax.broadcasted_iota(jnp.int32, (Q_ROWS, nk), 1)
        mask = (key >= n * dec_seq) & (key <= n * dec_seq + t)
        for h in range(H_B):
            k_h = knew_ref[:, h * dh:(h + 1) * dh].astype(BF16)
            s = lax.dot_general(q_ref[hrows(h), :], k_h, nt_dims, preferred_element_type=F32)
            s = s * scale + cq_ref[hrows(h), :] - cknew_ref[h:h + 1, :]
            s_ref[hrows(h), 0:nk] = jnp.where(mask, s, NEG_INF)
        softmax_update(nk)
        for h in range(H_B):
            v_h = vnew_ref[:, h * dh:(h + 1) * dh].astype(BF16)
            acc_ref[hrows(h), :] += jnp.dot(p_ref[hrows(h), 0:nk], v_h, preferred_element_type=F32)

    for r in range(pp):
        cols = slice(r * PAGE_SIZE, (r + 1) * PAGE_SIZE)
        for h in range(H_B):
            k_h = kp_refs[r][pl.ds(h, PAGE_SIZE, stride=H_B), :].astype(BF16)
            s = lax.dot_general(q_ref[hrows(h), :], k_h, nt_dims, preferred_element_type=F32)
            s_ref[hrows(h), cols] = s * scale + cq_ref[hrows(h), :] - ckpast_ref[h:h + 1, cols]
    softmax_update(pp * PAGE_SIZE)
    for h in range(H_B):
        pv = None
        for r in range(pp):
            cols = slice(r * PAGE_SIZE, (r + 1) * PAGE_SIZE)
            v_h = vp_refs[r][pl.ds(h, PAGE_SIZE, stride=H_B), :].astype(BF16)
            d = jnp.dot(p_ref[hrows(h), cols], v_h, preferred_element_type=F32)
            pv = d if pv is None else pv + d
        acc_ref[hrows(h), :] += pv

    @pl.when(pg == pl.num_programs(1) - 1)
    def _():
        o_ref[...] = acc_ref[...] / l_ref[...]


def _fox_decode(page_table, q_s, cq_s, k_new, v_new, ck_new, ck_past, cache_k_rows, cache_v_rows, layer, n_pool,
                scale, dec_seq):
    n, n_pages = page_table.shape
    rows, dh = q_s.shape[1:]
    d = k_new.shape[1]
    nk = k_new.shape[0]
    pp = PAGES_PER_STEP
    steps = n_pages // pp
    page_rows = PAGE_SIZE * H_B

    def page_spec(r):
        return pl.BlockSpec((page_rows, dh), lambda b, p, pt: (layer * n_pool + pt[b, p * pp + r], 0))

    in_specs = [
        pl.BlockSpec((None, rows, dh), lambda b, p, pt: (b, 0, 0)),
        pl.BlockSpec((None, rows, 1), lambda b, p, pt: (b, 0, 0)),
        pl.BlockSpec((nk, d), lambda b, p, pt: (0, 0)),
        pl.BlockSpec((nk, d), lambda b, p, pt: (0, 0)),
        pl.BlockSpec((H_B, nk), lambda b, p, pt: (0, 0)),
        pl.BlockSpec((None, H_B, pp * PAGE_SIZE), lambda b, p, pt: (b, 0, p)),
    ]
    in_specs += [page_spec(r) for r in range(pp)]
    in_specs += [page_spec(r) for r in range(pp)]
    grid_spec = pltpu.PrefetchScalarGridSpec(
        num_scalar_prefetch=1,
        grid=(n, steps),
        in_specs=in_specs,
        out_specs=pl.BlockSpec((None, rows, dh), lambda b, p, pt: (b, 0, 0)),
        scratch_shapes=[pltpu.VMEM((rows, pp * PAGE_SIZE), F32), pltpu.VMEM((rows, pp * PAGE_SIZE), BF16),
                        pltpu.VMEM((rows, 1), F32), pltpu.VMEM((rows, 1), F32), pltpu.VMEM((rows, dh), F32)],
    )
    return pl.pallas_call(
        functools.partial(_fox_decode_kernel, scale=scale, dec_seq=dec_seq),
        grid_spec=grid_spec,
        out_shape=jax.ShapeDtypeStruct((n, rows, dh), F32),
        compiler_params=_cparams("arbitrary", "arbitrary"),
        name="fox_decode",
    )(page_table, q_s, cq_s, k_new, v_new, ck_new, ck_past,
      *([cache_k_rows] * pp), *([cache_v_rows] * pp))


def _router_kernel(x_ref, w_ref, b_ref, comb_ref, mask_ref):
    logits = jnp.dot(x_ref[...], w_ref[...].astype(BF16), preferred_element_type=F32) + b_ref[...]
    ne = logits.shape[1]
    col = lax.broadcasted_iota(jnp.int32, logits.shape, 1)
    m1 = jnp.max(logits, axis=-1, keepdims=True)
    i1 = jnp.min(jnp.where(logits == m1, col, ne), axis=-1, keepdims=True)
    rest = jnp.where(col == i1, -jnp.inf, logits)
    m2 = jnp.max(rest, axis=-1, keepdims=True)
    i2 = jnp.min(jnp.where(rest == m2, col, ne), axis=-1, keepdims=True)
    e2 = jnp.exp(m2 - m1)
    den = 1.0 + e2
    comb_ref[...] = jnp.where(col == i1, 1.0 / den, 0.0) + jnp.where(col == i2, e2 / den, 0.0)
    mask_ref[...] = ((col == i1) | (col == i2)).astype(jnp.int32)


def _router(xb, w_r, b_r, layer):
    mp, d = xb.shape
    ne = w_r.shape[-1]
    return pl.pallas_call(
        _router_kernel,
        grid=(mp // TM,),
        in_specs=[
            pl.BlockSpec((TM, d), lambda i: (i, 0)),
            pl.BlockSpec((None, d, ne), lambda i: (layer, 0, 0)),
            pl.BlockSpec((1, ne), lambda i: (0, 0)),
        ],
        out_specs=[pl.BlockSpec((TM, ne), lambda i: (i, 0)), pl.BlockSpec((TM, ne), lambda i: (i, 0))],
        out_shape=[jax.ShapeDtypeStruct((mp, ne), F32), jax.ShapeDtypeStruct((mp, ne), jnp.int32)],
        compiler_params=_cparams("arbitrary"),
        name="router",
    )(xb, w_r, b_r)


def _row_copy(src, dst, src_row, dst_row, sem):
    return pltpu.make_async_copy(src.at[pl.ds(src_row, 1)], dst.at[pl.ds(dst_row, 1)], sem)


def _dispatch_kernel(pa_ref, pb_ref, x_ref, xs_in_ref, xs_ref, buf_ref, sem_ref):
    del xs_in_ref
    i = pl.program_id(0)
    slot = i % 2
    half = x_ref.shape[1] // 2
    buf = buf_ref.at[slot]
    sem = sem_ref.at[slot]

    def drain(b, s):
        def body(r, c):
            _row_copy(b, xs_ref, 0, 0, s).wait()
            return c
        lax.fori_loop(0, 2 * TM, body, 0)

    @pl.when(i >= 2)
    def _():
        drain(buf, sem)

    x = x_ref[...]
    for c in range(half // LANES):
        hi = x[:, c * LANES:(c + 1) * LANES].astype(BF16).astype(F32)
        lo = x[:, half + c * LANES:half + (c + 1) * LANES].astype(BF16).astype(F32)
        buf[:, c, :] = pltpu.bitcast(hi, U32) | (pltpu.bitcast(lo, U32) >> 16)

    def send(r, c):
        t = i * TM + r
        _row_copy(buf, xs_ref, r, pa_ref[t], sem).start()
        _row_copy(buf, xs_ref, r, pb_ref[t], sem).start()
        return c

    lax.fori_loop(0, TM, send, 0)

    @pl.when(i == pl.num_programs(0) - 1)
    def _():
        drain(buf, sem)

        @pl.when(i >= 1)
        def _():
            drain(buf_ref.at[1 - slot], sem_ref.at[1 - slot])


def _dispatch(pos_a, pos_b, x, n_rows):
    mp, d = x.shape
    words = d // 2
    assert words % LANES == 0
    xs0 = jnp.zeros((n_rows, words // LANES, LANES), U32)
    grid_spec = pltpu.PrefetchScalarGridSpec(
        num_scalar_prefetch=2,
        grid=(mp // TM,),
        in_specs=[pl.BlockSpec((TM, d), lambda i, pa, pb: (i, 0)), pl.BlockSpec(memory_space=pl.ANY)],
        out_specs=pl.BlockSpec(memory_space=pl.ANY),
        scratch_shapes=[pltpu.VMEM((2, TM, words // LANES, LANES), U32), pltpu.SemaphoreType.DMA((2,))],
    )
    return pl.pallas_call(
        _dispatch_kernel,
        grid_spec=grid_spec,
        out_shape=jax.ShapeDtypeStruct(xs0.shape, U32),
        input_output_aliases={3: 0},
        compiler_params=_cparams("arbitrary"),
        name="moe_dispatch",
    )(pos_a, pos_b, x, xs0)


def _expert_mlp_kernel(te_ref, tv_ref, tb_ref, xs_ref, wg_ref, wu_ref, wd_ref, o_ref, xb_ref, wgb_ref, wub_ref,
                       wdb_ref):
    i = pl.program_id(0)
    f = pl.program_id(1)
    valid = tv_ref[i]
    tg = o_ref.shape[0]
    sb = tg // N_SUB
    half = o_ref.shape[1] // 2

    @pl.when((f == 0) & (valid > 0))
    def _():
        for c in range(half // LANES):
            u = xs_ref[:, c, :]
            xb_ref[:, c * LANES:(c + 1) * LANES] = pltpu.bitcast(u & jnp.uint32(0xFFFF0000), F32).astype(BF16)
            xb_ref[:, half + c * LANES:half + (c + 1) * LANES] = pltpu.bitcast(u << 16, F32).astype(BF16)
        o_ref[...] = jnp.zeros_like(o_ref)

    @pl.when((f == 0) & (valid == 0))
    def _():
        o_ref[...] = jnp.zeros_like(o_ref)

    @pl.when(valid > 0)
    def _():
        wgb_ref[...] = wg_ref[...].astype(BF16)
        wub_ref[...] = wu_ref[...].astype(BF16)
        wdb_ref[...] = wd_ref[...].astype(BF16)

    for s in range(N_SUB):
        @pl.when(valid > s * sb)
        def _():
            rows = slice(s * sb, (s + 1) * sb)
            x = xb_ref[rows, :]
            g = jnp.dot(x, wgb_ref[...], preferred_element_type=F32)
            u = jnp.dot(x, wub_ref[...], preferred_element_type=F32)
            h = (jax.nn.silu(g) * u).astype(BF16)
            o_ref[rows, :] += jnp.dot(h, wdb_ref[...], preferred_element_type=F32)


def _expert_mlp(te, tv, tb, xs, w_gate, w_up, w_down, layer, tg):
    n_tiles = te.shape[0]
    d, ff = w_gate.shape[-2:]
    nf = ff // TF
    fcol = lambda i, f, tv: jnp.where(tv[i] > 0, f, nf - 1)
    grid_spec = pltpu.PrefetchScalarGridSpec(
        num_scalar_prefetch=3,
        grid=(n_tiles, nf),
        in_specs=[
            pl.BlockSpec((tg,) + xs.shape[1:], lambda i, f, te, tv, tb: (tb[i], 0, 0)),
            pl.BlockSpec((None, None, d, TF), lambda i, f, te, tv, tb: (layer, te[i], 0, fcol(i, f, tv))),
            pl.BlockSpec((None, None, d, TF), lambda i, f, te, tv, tb: (layer, te[i], 0, fcol(i, f, tv))),
            pl.BlockSpec((None, None, TF, d), lambda i, f, te, tv, tb: (layer, te[i], fcol(i, f, tv), 0)),
        ],
        out_specs=pl.BlockSpec((tg, d), lambda i, f, te, tv, tb: (i, 0)),
        scratch_shapes=[pltpu.VMEM((tg, d), BF16), pltpu.VMEM((d, TF), BF16), pltpu.VMEM((d, TF), BF16),
                        pltpu.VMEM((TF, d), BF16)],
    )
    return pl.pallas_call(
        _expert_mlp_kernel,
        grid_spec=grid_spec,
        out_shape=jax.ShapeDtypeStruct((n_tiles * tg, d), F32),
        compiler_params=_cparams("arbitrary", "arbitrary"),
        name="expert_mlp",
    )(te, tv, tb, xs, w_gate, w_up, w_down)


def _combine_ln_kernel(pa_ref, pb_ref, r_ref, ga_ref, gb_ref, g_ref, b_ref, y_ref, o_ref, ob_ref, ya_ref, yb_ref,
                       sem_ref, *, alpha):
    i = pl.program_id(0)
    n = pl.num_programs(0)
    slot = i % 2

    def fetch(tile, s):
        def body(r, c):
            t = tile * TC + r
            _row_copy(y_ref, ya_ref.at[s], pa_ref[t], r, sem_ref.at[s]).start()
            _row_copy(y_ref, yb_ref.at[s], pb_ref[t], r, sem_ref.at[s]).start()
            return c
        lax.fori_loop(0, TC, body, 0)

    @pl.when(i == 0)
    def _():
        fetch(0, 0)

    @pl.when(i + 1 < n)
    def _():
        fetch(i + 1, 1 - slot)

    def drain(r, c):
        _row_copy(y_ref, ya_ref.at[slot], 0, 0, sem_ref.at[slot]).wait()
        return c

    lax.fori_loop(0, 2 * TC, drain, 0)
    moe = ga_ref[...] * ya_ref[slot] + gb_ref[...] * yb_ref[slot]
    y = _layer_norm_rows(alpha * r_ref[...] + moe, g_ref[...], b_ref[...])
    o_ref[...] = y
    ob_ref[...] = y.astype(BF16)


def _combine_ln(pos_a, pos_b, resid, gate_a, gate_b, g, b, y, alpha):
    mp, d = resid.shape
    row = lambda i, pa, pb: (i, 0)
    fixed = lambda i, pa, pb: (0, 0)
    grid_spec = pltpu.PrefetchScalarGridSpec(
        num_scalar_prefetch=2,
        grid=(mp // TC,),
        in_specs=[
            pl.BlockSpec((TC, d), row),
            pl.BlockSpec((TC, 1), row),
            pl.BlockSpec((TC, 1), row),
            pl.BlockSpec((1, d), fixed),
            pl.BlockSpec((1, d), fixed),
            pl.BlockSpec(memory_space=pl.ANY),
        ],
        out_specs=[pl.BlockSpec((TC, d), row), pl.BlockSpec((TC, d), row)],
        scratch_shapes=[pltpu.VMEM((2, TC, d), F32), pltpu.VMEM((2, TC, d), F32), pltpu.SemaphoreType.DMA((2,))],
    )
    return pl.pallas_call(
        functools.partial(_combine_ln_kernel, alpha=alpha),
        grid_spec=grid_spec,
        out_shape=[jax.ShapeDtypeStruct((mp, d), F32), jax.ShapeDtypeStruct((mp, d), BF16)],
        compiler_params=_cparams("arbitrary"),
        name="moe_combine_ln",
    )(pos_a, pos_b, resid, gate_a, gate_b, g, b, y)


def _expert_row_tile(m_real, ne):
    target = -(-(TOP_K * m_real * 21) // (ne * 20))
    n_split = max(1, (target + 512) // 1024)
    unit = SUBLANES * 2 * N_SUB
    return -(-target // (n_split * unit)) * unit


def _moe(x, xb, m_real, w_router, b_router, w_gate, w_up, w_down, layer, ln_g, ln_b, alpha):
    mp, d = x.shape
    ne = w_router.shape[-1]
    tg = _expert_row_tile(m_real, ne)
    comb, mask = _router(xb, w_router, b_router, layer)
    live = (jnp.arange(mp, dtype=jnp.int32) < m_real)[:, None]
    mask = jnp.where(live, mask, 0)
    col = jnp.arange(ne, dtype=jnp.int32)[None, :]
    rank = jnp.cumsum(mask, axis=0) - mask
    counts = jnp.sum(mask, axis=0)
    tiles_e = (counts + tg - 1) // tg
    tile_end = jnp.cumsum(tiles_e)
    tile_start = tile_end - tiles_e
    n_tiles = (TOP_K * m_real) // tg + ne
    p = n_tiles * tg
    pos = tile_start[None, :] * tg + rank
    e_hi = jnp.max(jnp.where(mask > 0, col, -1), axis=1, keepdims=True)
    e_lo = jnp.min(jnp.where(mask > 0, col, ne), axis=1, keepdims=True)
    pos_a = jnp.sum(jnp.where(col == e_hi, pos, 0), axis=1)
    pos_b = jnp.sum(jnp.where(col == e_lo, pos, 0), axis=1)
    gate_a = jnp.sum(jnp.where(col == e_hi, comb, 0.0), axis=1, keepdims=True)
    gate_b = jnp.sum(jnp.where(col == e_lo, comb, 0.0), axis=1, keepdims=True)
    spare = p + 2 * (jnp.arange(mp, dtype=jnp.int32) - m_real)
    send_a = jnp.where(live[:, 0], pos_a, spare)
    send_b = jnp.where(live[:, 0], pos_b, spare + 1)
    tile_id = jnp.arange(n_tiles, dtype=jnp.int32)
    n_active = tile_end[-1]
    tile_c = jnp.minimum(tile_id, jnp.maximum(n_active - 1, 0))
    te = jnp.minimum(jnp.sum((tile_end[None, :] <= tile_c[:, None]).astype(jnp.int32), axis=1), ne - 1)
    tv = jnp.clip(counts[te] - (tile_c - tile_start[te]) * tg, 0, tg)
    tv = jnp.where(tile_id < n_active, tv, 0).astype(jnp.int32)
    tb = tile_c.astype(jnp.int32)

    xs = _dispatch(send_a.astype(jnp.int32), send_b.astype(jnp.int32), x, p + 2 * (mp - m_real))
    y = _expert_mlp(te.astype(jnp.int32), tv, tb, xs, w_gate, w_up, w_down, layer, tg)
    return _combine_ln(pos_a.astype(jnp.int32), pos_b.astype(jnp.int32), x, gate_a, gate_b, ln_g, ln_b, y, alpha)


def kernel(x_prompt, x_sample, cache_k, cache_v, cache_logf, state_conv, page_table, a_w_in, a_ln_g, a_ln_b, a_w_s, a_b_s, a_w_out, b_w_in, b_b_f, b_w_o, c_w_in, c_conv_w, c_w_out, ffn_w_gate, ffn_w_up, ffn_w_down, moe_w_router, moe_b_router, moe_w_gate, moe_w_up, moe_w_down, ln1_g, ln1_b, ln2_g, ln2_b):
    batch, seq, d = x_prompt.shape
    dec_batch, dec_seq, _ = x_sample.shape
    depth = ln1_g.shape[0]
    alpha = (2.0 * depth) ** 0.25
    m_prompt = batch * seq
    m_samp = dec_batch * dec_seq
    m_real = m_prompt + m_samp
    assert seq % TM == 0 and m_samp <= CHUNK and TM % CHUNK == 0 and dec_seq <= SUBLANES
    n_prompt_tiles = m_prompt // TM
    tiles_per_seq = seq // TM
    mp = m_prompt + TM
    n_pool = cache_k.shape[1]
    n_pages = page_table.shape[1]
    assert n_pages % PAGES_PER_STEP == 0 and n_pages % SUFFIX_PAGES_PER_STEP == 0
    dh = d // H_B
    e_a = a_w_out.shape[1]
    e_c = c_w_out.shape[1]
    g_a = e_a // H_A
    assert e_a == d and e_c == d and g_a % LANES == 0 and dh == LANES

    x = jnp.concatenate([x_prompt.reshape(m_prompt, d), x_sample.reshape(m_samp, d),
                         jnp.zeros((mp - m_real, d), F32)], axis=0)
    xb = x.astype(BF16)

    r = jnp.arange(TM)
    tri_prompt = r[:, None] >= r[None, :]
    tri_tail = tri_prompt & ((r[:, None] // dec_seq) == (r[None, :] // dec_seq))
    tri = jnp.stack([tri_prompt, tri_tail]).astype(BF16)
    rp = jnp.arange(PAGE_SIZE)
    tri_upper = (rp[None, :] > rp[:, None]).astype(BF16)

    k_p, v_p, lf_p, k_s, v_s, lf_s, conv_p, conv_s, chunk_v_s = [], [], [], [], [], [], [], [], []
    n_mix = 3
    for li in range(depth):
        kind = li % n_mix
        j = li // n_mix
        g1, b1 = ln1_g[li][None], ln1_b[li][None]
        g2, b2 = ln2_g[li][None], ln2_b[li][None]
        if kind == 0:
            nt = e_a // TN
            (u,) = _proj(xb, [(a_w_in, j, 0)], functools.partial(_ep_act, jax.nn.gelu), [_tile_spec()],
                         [jax.ShapeDtypeStruct((mp, e_a), BF16)], n_col_tiles=nt, name="a_in_u")
            (v,) = _proj(xb, [(a_w_in, j, nt)], functools.partial(_ep_act, jax.nn.gelu), [_tile_spec()],
                         [jax.ShapeDtypeStruct((mp, e_a), F32)], n_col_tiles=nt, name="a_in_v")
            rc = jnp.arange(CHUNK)
            causal = rc[:, None] >= rc[None, :]
            w_prompt = jnp.where(causal[None], a_w_s[j], 0.0)
            same = (rc[:, None] // dec_seq) == (rc[None, :] // dec_seq)
            t_in = rc % dec_seq
            w_tail = jnp.where((causal & same)[None], a_w_s[j][:, t_in][:, :, t_in], 0.0)
            wmix = jnp.stack([w_prompt, w_tail]).astype(BF16)
            bias_p = jnp.repeat(a_b_s[j].T, g_a, axis=1)
            bias_t = jnp.repeat(a_b_s[j][:, t_in].T, g_a, axis=1)
            bias = jnp.stack([bias_p, bias_t])
            us, vs = _spatial(u, v, a_ln_g[j][None], a_ln_b[j][None], wmix, bias, n_prompt_tiles)
            chunk_v_s.append(vs[:m_samp].reshape(dec_batch, dec_seq, e_a))
            x, xb = _out_ln(us, a_w_out[j].astype(BF16), x, g1, b1, alpha, TM)
        elif kind == 1:
            nt = d // TN
            ident = functools.partial(_ep_act, lambda a: a)
            (q16,) = _proj(xb, [(b_w_in, j, 0)], ident, [_tile_spec()],
                           [jax.ShapeDtypeStruct((mp, d), BF16)], n_col_tiles=nt, name="b_in_q")
            k32, k16 = _proj(xb, [(b_w_in, j, nt)], ident, [_tile_spec(), _tile_spec()],
                             [jax.ShapeDtypeStruct((mp, d), F32), jax.ShapeDtypeStruct((mp, d), BF16)],
                             n_col_tiles=nt, name="b_in_k")
            v32, v16 = _proj(xb, [(b_w_in, j, 2 * nt)], ident, [_tile_spec(), _tile_spec()],
                             [jax.ShapeDtypeStruct((mp, d), F32), jax.ShapeDtypeStruct((mp, d), BF16)],
                             n_col_tiles=nt, name="b_in_v")
            w_f = b_w_in[j][:, 3 * d:]
            lf, c = _logf(xb, w_f, b_b_f[j][None, :], tri, n_prompt_tiles, tiles_per_seq)
            scale = dh ** -0.5
            c_t = c.T
            cq = c_t[:, :m_prompt, None]
            ck = c_t[:, None, :m_prompt]
            o_prompt = _fox_prompt(q16, k16, v16, cq, ck, batch, seq, scale)

            cache_k_rows = cache_k.reshape(-1, dh)
            cache_v_rows = cache_v.reshape(-1, dh)
            cache_lf_rows = cache_logf.reshape(-1, H_B)
            suffix = _suffix(page_table, cache_lf_rows, j, n_pool, tri_upper)
            ck_past = -suffix.reshape(dec_batch, n_pages * PAGE_SIZE, H_B).transpose(0, 2, 1)
            qpad = ((0, 0), (0, 0), (0, Q_ROWS - dec_seq), (0, 0))
            q_s = q16[m_prompt:m_real].reshape(dec_batch, dec_seq, H_B, dh).transpose(0, 2, 1, 3)
            q_s = jnp.pad(q_s, qpad).reshape(dec_batch, H_B * Q_ROWS, dh)
            c_s = c[m_prompt:m_real].reshape(dec_batch, dec_seq, H_B, 1).transpose(0, 2, 1, 3)
            cq_s = jnp.pad(c_s, qpad).reshape(dec_batch, H_B * Q_ROWS, 1)
            o_samp = _fox_decode(page_table, q_s, cq_s, k32[m_prompt:m_prompt + CHUNK],
                                 v32[m_prompt:m_prompt + CHUNK], c_t[:, m_prompt:m_prompt + CHUNK], ck_past,
                                 cache_k_rows, cache_v_rows, j, n_pool, scale, dec_seq)
            o_samp = o_samp.reshape(dec_batch, H_B, Q_ROWS, dh)[:, :, :dec_seq].transpose(0, 2, 1, 3)
            o_samp = o_samp.reshape(m_samp, d).astype(BF16)
            o_all = jnp.concatenate([o_prompt, o_samp, jnp.zeros((mp - m_real, d), BF16)], axis=0)
            x, xb = _out_ln(o_all, b_w_o[j].astype(BF16), x, g1, b1, alpha, TM)
            k_p.append(k32[:m_prompt].reshape(batch, seq, H_B, dh))
            v_p.append(v32[:m_prompt].reshape(batch, seq, H_B, dh))
            lf_p.append(lf[:m_prompt].reshape(batch, seq, H_B))
            k_s.append(k32[m_prompt:m_real].reshape(dec_batch, dec_seq, H_B, dh))
            v_s.append(v32[m_prompt:m_real].reshape(dec_batch, dec_seq, H_B, dh))
            lf_s.append(lf[m_prompt:m_real].reshape(dec_batch, dec_seq, H_B))
        else:
            nt = e_c // TN
            st = state_conv[j]
            p1 = jnp.zeros((dec_batch, dec_seq, e_c), F32).at[:, 0].set(st[:, 1])
            p2 = jnp.zeros((dec_batch, dec_seq, e_c), F32).at[:, 0].set(st[:, 0]).at[:, 1].set(st[:, 1])
            pad = jnp.zeros((TM - m_samp, e_c), F32)
            p1 = jnp.concatenate([p1.reshape(m_samp, e_c), pad], axis=0)
            p2 = jnp.concatenate([p2.reshape(m_samp, e_c), pad], axis=0)
            n_tiles = mp // TM
            ep = functools.partial(_ep_conv, tiles_per_seq=tiles_per_seq, n_prompt_tiles=n_prompt_tiles,
                                   dec_seq=dec_seq)
            bgy, ztail, zs = _proj(
                xb, [(c_w_in, j, 0), (c_w_in, j, nt), (c_w_in, j, 2 * nt)], ep,
                [_tile_spec(), pl.BlockSpec((SUBLANES, TN), lambda jj, i: (i, jj)),
                 pl.BlockSpec((TM, TN), lambda jj, i: (0, jj))],
                [jax.ShapeDtypeStruct((mp, e_c), BF16), jax.ShapeDtypeStruct((n_tiles * SUBLANES, e_c), F32),
                 jax.ShapeDtypeStruct((TM, e_c), F32)],
                n_col_tiles=nt, name="c_in_conv",
                extra_in=[c_conv_w, p1, p2],
                extra_specs=[pl.BlockSpec((None, CONV_W, TN), lambda jj, i: (j, 0, jj)),
                             pl.BlockSpec((TM, TN), lambda jj, i: (0, jj)),
                             pl.BlockSpec((TM, TN), lambda jj, i: (0, jj))],
                extra_scratch=[pltpu.VMEM((SUBLANES, TN), F32)])
            zt = ztail.reshape(n_tiles, SUBLANES, e_c)
            last_tiles = jnp.arange(batch) * tiles_per_seq + tiles_per_seq - 1
            conv_p.append(zt[last_tiles][:, SUBLANES - (CONV_W - 1):, :])
            conv_s.append(zs[:m_samp].reshape(dec_batch, dec_seq, e_c)[:, dec_seq - (CONV_W - 1):, :])
            x, xb = _out_ln(bgy, c_w_out[j].astype(BF16), x, g1, b1, alpha, TM)

        f = li // 2
        if li % 2 == 0:
            nt = ffn_w_gate.shape[-1] // TN
            (h,) = _proj(xb, [(ffn_w_gate, f, 0), (ffn_w_up, f, 0)], _ep_swiglu, [_tile_spec()],
                         [jax.ShapeDtypeStruct((mp, ffn_w_gate.shape[-1]), BF16)], n_col_tiles=nt, name="ffn_up")
            x, xb = _out_ln(h, ffn_w_down[f].astype(BF16), x, g2, b2, alpha, TM_WIDE_K)
        else:
            x, xb = _moe(x, xb, m_real, moe_w_router, moe_b_router[f][None], moe_w_gate, moe_w_up, moe_w_down, f,
                         g2, b2, alpha)

    y_prompt = x[:m_prompt].reshape(batch, seq, d)
    y_sample = x[m_prompt:m_real].reshape(dec_batch, dec_seq, d)
    return (y_prompt, y_sample, jnp.stack(k_p), jnp.stack(v_p), jnp.stack(lf_p), jnp.stack(k_s),
            jnp.stack(v_s), jnp.stack(lf_s), jnp.stack(conv_p), jnp.stack(conv_s), jnp.stack(chunk_v_s))
```

```python
import functools

import jax
import jax.numpy as jnp
from jax import lax
from jax.experimental import pallas as pl
from jax.experimental.pallas import tpu as pltpu

F32 = jnp.float32
BF16 = jnp.bfloat16
U32 = jnp.uint32

H_A = 16
CHUNK = 128
H_B = 16
PAGE_SIZE = 128
CONV_W = 3
TOP_K = 2
LN_EPS = 1e-5
NEG_INF = -1e30

SUBLANES = 8
LANES = 128

TM = 512
TN = 512
TM_WIDE_K = 256
TKV = 512
HEADS_PER_STEP = 2
PAGES_PER_STEP = 4
SUFFIX_PAGES_PER_STEP = 8
Q_ROWS = 16
TF = 256
N_SUB = 4
TC = 256
DMA_UNROLL = 8

VMEM_LIMIT = 56 * 1024 * 1024


def _cparams(*sem):
    return pltpu.CompilerParams(dimension_semantics=sem, vmem_limit_bytes=VMEM_LIMIT)


def _layer_norm_rows(y, g, b):
    mu = jnp.mean(y, axis=-1, keepdims=True)
    d = y - mu
    var = jnp.mean(d * d, axis=-1, keepdims=True)
    return d * lax.rsqrt(var + LN_EPS) * g + b


def _proj_kernel(*refs, n_w, n_extra, n_out, epilogue):
    x_ref = refs[0]
    w_refs = refs[1:1 + n_w]
    extra = refs[1 + n_w:1 + n_w + n_extra]
    outs = refs[1 + n_w + n_extra:1 + n_w + n_extra + n_out]
    scratch = refs[1 + n_w + n_extra + n_out:]
    wb_refs = scratch[:n_w]
    rest = scratch[n_w:]

    @pl.when(pl.program_id(1) == 0)
    def _():
        for w_ref, wb_ref in zip(w_refs, wb_refs):
            wb_ref[...] = w_ref[...].astype(BF16)

    x = x_ref[...]
    accs = [jnp.dot(x, wb_ref[...], preferred_element_type=F32) for wb_ref in wb_refs]
    epilogue(accs, extra, outs, rest)


def _proj(x, weights, epilogue, out_specs, out_shapes, *, n_col_tiles, name, extra_in=(), extra_specs=(),
          extra_scratch=(), tn=TN):
    mp, k = x.shape
    n_w = len(weights)
    in_specs = [pl.BlockSpec((TM, k), lambda j, i: (i, 0))]
    args = [x]
    for w, layer, off in weights:
        in_specs.append(pl.BlockSpec((None, k, tn), functools.partial(
            lambda j, i, layer, off: (layer, 0, j + off), layer=layer, off=off)))
        args.append(w)
    in_specs += list(extra_specs)
    args += list(extra_in)
    kern = functools.partial(_proj_kernel, n_w=n_w, n_extra=len(extra_in), n_out=len(out_shapes),
                             epilogue=epilogue)
    return pl.pallas_call(
        kern,
        grid=(n_col_tiles, mp // TM),
        in_specs=in_specs,
        out_specs=out_specs,
        out_shape=out_shapes,
        scratch_shapes=[pltpu.VMEM((k, tn), BF16) for _ in range(n_w)] + list(extra_scratch),
        compiler_params=_cparams("arbitrary", "arbitrary"),
        name=name,
    )(*args)


def _tile_spec(tn=TN):
    return pl.BlockSpec((TM, tn), lambda j, i: (i, j))


def _ep_gelu(accs, extra, outs, rest):
    for acc, o in zip(accs, outs):
        o[...] = jax.nn.gelu(acc).astype(o.dtype)


def _ep_qkv(accs, extra, outs, rest, *, n_prompt_tiles):
    q16, k16, v16, k32p, k32t, v32p, v32t = outs
    q16[...] = accs[0].astype(BF16)
    k16[...] = accs[1].astype(BF16)
    v16[...] = accs[2].astype(BF16)
    i = pl.program_id(1)

    @pl.when(i < n_prompt_tiles)
    def _():
        k32p[...] = accs[1]
        v32p[...] = accs[2]

    @pl.when(i >= n_prompt_tiles)
    def _():
        k32t[...] = accs[1]
        v32t[...] = accs[2]


def _ep_swiglu(accs, extra, outs, rest):
    outs[0][...] = (jax.nn.silu(accs[0]) * accs[1]).astype(outs[0].dtype)


def _ep_conv(accs, extra, outs, rest, *, tiles_per_seq, n_prompt_tiles, dec_seq):
    cw_ref, p1_ref, p2_ref = extra
    y_ref, ztail_ref, zs_ref = outs
    (carry_ref,) = rest
    i = pl.program_id(1)
    z = accs[1] * accs[2]
    row = lax.broadcasted_iota(jnp.int32, z.shape, 0)
    r1 = pltpu.roll(z, 1, 0)
    r2 = pltpu.roll(z, 2, 0)
    w0 = cw_ref[0:1, :]
    w1 = cw_ref[1:2, :]
    w2 = cw_ref[2:3, :]

    @pl.when(i % tiles_per_seq == 0)
    def _():
        carry_ref[...] = jnp.zeros_like(carry_ref)

    @pl.when(i < n_prompt_tiles)
    def _():
        c6 = carry_ref[6:7, :]
        c7 = carry_ref[7:8, :]
        s1 = jnp.where(row == 0, c7, r1)
        s2 = jnp.where(row == 0, c6, jnp.where(row == 1, c7, r2))
        y = w0 * s2 + w1 * s1 + w2 * z
        y_ref[...] = (accs[0] * y).astype(y_ref.dtype)

    @pl.when(i >= n_prompt_tiles)
    def _():
        t = row % dec_seq
        s1 = jnp.where(t >= 1, r1, p1_ref[...])
        s2 = jnp.where(t >= 2, r2, p2_ref[...])
        y = w0 * s2 + w1 * s1 + w2 * z
        y_ref[...] = (accs[0] * y).astype(y_ref.dtype)
        zs_ref[...] = z

    carry_ref[...] = z[TM - SUBLANES:, :]
    ztail_ref[...] = z[TM - SUBLANES:, :]


def _out_ln_kernel(x_ref, w_ref, r_ref, g_ref, b_ref, o_ref, ob_ref, *, alpha):
    acc = jnp.dot(x_ref[...], w_ref[...], preferred_element_type=F32)
    y = _layer_norm_rows(alpha * r_ref[...] + acc, g_ref[...], b_ref[...])
    o_ref[...] = y
    ob_ref[...] = y.astype(BF16)


def _out_ln(x, w, resid, g, b, alpha, tm):
    mp, k = x.shape
    d = resid.shape[1]
    row = lambda i: (i, 0)
    fixed = lambda i: (0, 0)
    return pl.pallas_call(
        functools.partial(_out_ln_kernel, alpha=alpha),
        grid=(mp // tm,),
        in_specs=[
            pl.BlockSpec((tm, k), row),
            pl.BlockSpec((k, d), fixed, pipeline_mode=pl.Buffered(1)),
            pl.BlockSpec((tm, d), row),
            pl.BlockSpec((1, d), fixed),
            pl.BlockSpec((1, d), fixed),
        ],
        out_specs=[pl.BlockSpec((tm, d), row), pl.BlockSpec((tm, d), row)],
        out_shape=[jax.ShapeDtypeStruct((mp, d), F32), jax.ShapeDtypeStruct((mp, d), BF16)],
        compiler_params=_cparams("arbitrary"),
        name="out_ln",
    )(x, w, resid, g, b)


def _spatial_kernel(u_ref, v_ref, g_ref, b_ref, wmix_ref, bias_ref, o_ref, vs_ref, vln_ref, *, n_prompt_tiles):
    i = pl.program_id(0)
    vln = _layer_norm_rows(v_ref[...], g_ref[...], b_ref[...])
    vln_ref[...] = vln.astype(BF16)

    @pl.when(i >= n_prompt_tiles)
    def _():
        vs_ref[...] = vln

    g_a = u_ref.shape[1] // H_A

    def chunk_body(c, carry):
        r0 = pl.multiple_of(c * CHUNK, CHUNK)
        for h in range(H_A):
            cols = slice(h * g_a, (h + 1) * g_a)
            s = jnp.dot(wmix_ref[h], vln_ref[pl.ds(r0, CHUNK), cols], preferred_element_type=F32)
            s = s + bias_ref[:, cols]
            o_ref[pl.ds(r0, CHUNK), cols] = (u_ref[pl.ds(r0, CHUNK), cols].astype(F32) * s).astype(o_ref.dtype)
        return carry

    lax.fori_loop(0, TM // CHUNK, chunk_body, 0)


def _spatial(u, v, ln_g, ln_b, wmix, bias, n_prompt_tiles):
    mp, e = u.shape
    sel = lambda i: jnp.where(i < n_prompt_tiles, 0, 1)
    return pl.pallas_call(
        functools.partial(_spatial_kernel, n_prompt_tiles=n_prompt_tiles),
        grid=(mp // TM,),
        in_specs=[
            pl.BlockSpec((TM, e), lambda i: (i, 0)),
            pl.BlockSpec((TM, e), lambda i: (i, 0)),
            pl.BlockSpec((1, e), lambda i: (0, 0)),
            pl.BlockSpec((1, e), lambda i: (0, 0)),
            pl.BlockSpec((None, H_A, CHUNK, CHUNK), lambda i: (sel(i), 0, 0, 0)),
            pl.BlockSpec((None, CHUNK, e), lambda i: (sel(i), 0, 0)),
        ],
        out_specs=[pl.BlockSpec((TM, e), lambda i: (i, 0)), pl.BlockSpec((TM, e), lambda i: (0, 0))],
        out_shape=[jax.ShapeDtypeStruct((mp, e), BF16), jax.ShapeDtypeStruct((TM, e), F32)],
        scratch_shapes=[pltpu.VMEM((TM, e), BF16)],
        compiler_params=_cparams("arbitrary"),
        name="spatial_mix",
    )(u, v, ln_g, ln_b, wmix, bias)


def _split3(x):
    hi = x.astype(BF16)
    r1 = x - hi.astype(F32)
    mid = r1.astype(BF16)
    lo = (r1 - mid.astype(F32)).astype(BF16)
    return hi, mid, lo


def _tri_matmul(tri, x):
    hi, mid, lo = _split3(x)
    out = jnp.dot(tri, lo, preferred_element_type=F32)
    out = out + jnp.dot(tri, mid, preferred_element_type=F32)
    return out + jnp.dot(tri, hi, preferred_element_type=F32)


def _logf_kernel(x_ref, w_ref, bf_ref, tri_ref, lf_ref, c_ref, carry_ref, *, tiles_per_seq):
    i = pl.program_id(0)

    @pl.when(i % tiles_per_seq == 0)
    def _():
        carry_ref[...] = jnp.zeros_like(carry_ref)

    z = jnp.dot(x_ref[...], w_ref[...].astype(BF16), preferred_element_type=F32) + bf_ref[...]
    lf = jax.nn.log_sigmoid(z)
    lf_ref[...] = lf
    c = _tri_matmul(tri_ref[...], lf) + carry_ref[0:1, :]
    c_ref[...] = c
    carry_ref[0:1, :] = c[TM - 1:TM, :]


def _logf(xb, w_f, b_f, tri, n_prompt_tiles, tiles_per_seq):
    mp, d = xb.shape
    h = w_f.shape[1]
    sel = lambda i: jnp.where(i < n_prompt_tiles, 0, 1)
    return pl.pallas_call(
        functools.partial(_logf_kernel, tiles_per_seq=tiles_per_seq),
        grid=(mp // TM,),
        in_specs=[
            pl.BlockSpec((TM, d), lambda i: (i, 0)),
            pl.BlockSpec((d, h), lambda i: (0, 0)),
            pl.BlockSpec((1, h), lambda i: (0, 0)),
            pl.BlockSpec((None, TM, TM), lambda i: (sel(i), 0, 0)),
        ],
        out_specs=[pl.BlockSpec((TM, h), lambda i: (i, 0)), pl.BlockSpec((TM, h), lambda i: (i, 0))],
        out_shape=[jax.ShapeDtypeStruct((mp, h), F32), jax.ShapeDtypeStruct((mp, h), F32)],
        scratch_shapes=[pltpu.VMEM((SUBLANES, h), F32)],
        compiler_params=_cparams("arbitrary"),
        name="logf_cumsum",
    )(xb, w_f, b_f, tri)


def _fox_prompt_kernel(q_ref, k_ref, v_ref, cq_ref, ck_ref, o_ref, *, scale):
    qi = pl.program_id(2)
    tq = q_ref.shape[0]
    dh = q_ref.shape[1] // HEADS_PER_STEP
    heads = range(HEADS_PER_STEP)
    qs = [q_ref[:, g * dh:(g + 1) * dh] for g in heads]
    cqs = [cq_ref[g] for g in heads]

    def step(g, kv0, carry, masked):
        m, l, acc = carry
        k = k_ref[pl.ds(kv0, TKV), g * dh:(g + 1) * dh]
        v = v_ref[pl.ds(kv0, TKV), g * dh:(g + 1) * dh]
        s = lax.dot_general(qs[g], k, (((1,), (1,)), ((), ())), preferred_element_type=F32)
        s = s * scale + cqs[g] - ck_ref[g, :, pl.ds(kv0, TKV)]
        if masked:
            row = lax.broadcasted_iota(jnp.int32, s.shape, 0)
            col = lax.broadcasted_iota(jnp.int32, s.shape, 1)
            s = jnp.where(row >= col, s, NEG_INF)
        m_new = jnp.maximum(m, jnp.max(s, axis=-1, keepdims=True))
        a = jnp.exp(m - m_new)
        p = jnp.exp(s - m_new)
        l = a * l + jnp.sum(p, axis=-1, keepdims=True)
        acc = a * acc + jnp.dot(p.astype(BF16), v, preferred_element_type=F32)
        return m_new, l, acc

    init = tuple((jnp.full((tq, 1), NEG_INF, F32), jnp.zeros((tq, 1), F32), jnp.zeros((tq, dh), F32))
                 for _ in heads)

    def body(j, carries):
        kv0 = pl.multiple_of(j * TKV, TKV)
        return tuple(step(g, kv0, carries[g], False) for g in heads)

    carries = lax.fori_loop(0, qi * (tq // TKV), body, init)
    for g in heads:
        m, l, acc = step(g, pl.multiple_of(qi * tq, tq), carries[g], True)
        o_ref[:, g * dh:(g + 1) * dh] = (acc / l).astype(o_ref.dtype)


def _fox_prompt(q, k, v, cq, ck, batch, seq, scale):
    dh = q.shape[1] // H_B
    nq = seq // TM
    w = HEADS_PER_STEP * dh
    return pl.pallas_call(
        functools.partial(_fox_prompt_kernel, scale=scale),
        grid=(batch, H_B // HEADS_PER_STEP, nq),
        in_specs=[
            pl.BlockSpec((TM, w), lambda b, h, i: (b * nq + i, h)),
            pl.BlockSpec((seq, w), lambda b, h, i: (b, h)),
            pl.BlockSpec((seq, w), lambda b, h, i: (b, h)),
            pl.BlockSpec((HEADS_PER_STEP, TM, 1), lambda b, h, i: (h, b * nq + i, 0)),
            pl.BlockSpec((HEADS_PER_STEP, 1, seq), lambda b, h, i: (h, 0, b)),
        ],
        out_specs=pl.BlockSpec((TM, w), lambda b, h, i: (b * nq + i, h)),
        out_shape=jax.ShapeDtypeStruct((batch * seq, H_B * dh), BF16),
        compiler_params=_cparams("arbitrary", "arbitrary", "arbitrary"),
        name="fox_prompt",
    )(q, k, v, cq, ck)


def _suffix_kernel(pt_ref, *refs):
    sp = SUFFIX_PAGES_PER_STEP
    lf_refs = refs[:sp]
    tri_ref, o_ref, carry_ref = refs[sp:]

    @pl.when(pl.program_id(1) == 0)
    def _():
        carry_ref[...] = jnp.zeros_like(carry_ref)

    carry = carry_ref[0:1, :]
    tri = tri_ref[...]
    for r in reversed(range(sp)):
        lf = lf_refs[r][...]
        o_ref[r] = _tri_matmul(tri, lf) + carry
        carry = carry + jnp.sum(lf, axis=0, keepdims=True)
    carry_ref[0:1, :] = carry


def _suffix(page_table, cache_lf_rows, layer, n_pool, tri_upper):
    n, n_pages = page_table.shape
    h = cache_lf_rows.shape[-1]
    sp = SUFFIX_PAGES_PER_STEP
    groups = n_pages // sp

    def page_spec(r):
        return pl.BlockSpec((PAGE_SIZE, h),
                            lambda b, p, pt: (layer * n_pool + pt[b, (groups - 1 - p) * sp + r], 0))

    grid_spec = pltpu.PrefetchScalarGridSpec(
        num_scalar_prefetch=1,
        grid=(n, groups),
        in_specs=[page_spec(r) for r in range(sp)] + [
            pl.BlockSpec((PAGE_SIZE, PAGE_SIZE), lambda b, p, pt: (0, 0))],
        out_specs=pl.BlockSpec((None, sp, PAGE_SIZE, h), lambda b, p, pt: (b, groups - 1 - p, 0, 0)),
        scratch_shapes=[pltpu.VMEM((SUBLANES, h), F32)],
    )
    return pl.pallas_call(
        _suffix_kernel,
        grid_spec=grid_spec,
        out_shape=jax.ShapeDtypeStruct((n, n_pages, PAGE_SIZE, h), F32),
        compiler_params=_cparams("arbitrary", "arbitrary"),
        name="logf_suffix",
    )(page_table, *([cache_lf_rows] * sp), tri_upper)


def _fox_decode_kernel(pt_ref, q_ref, cq_ref, knew_ref, vnew_ref, cknew_ref, ckpast_ref, *rest, scale, dec_seq):
    pp = PAGES_PER_STEP
    kp_refs = rest[:pp]
    vp_refs = rest[pp:2 * pp]
    o_ref = rest[2 * pp]
    s_ref, p_ref, m_ref, l_ref, acc_ref = rest[2 * pp + 1:]
    n = pl.program_id(0)
    pg = pl.program_id(1)
    dh = q_ref.shape[1]
    nt_dims = (((1,), (1,)), ((), ()))

    def hrows(h):
        return slice(h * Q_ROWS, (h + 1) * Q_ROWS)

    def softmax_update(width):
        s = s_ref[:, :width]
        m_old = m_ref[...]
        m_new = jnp.maximum(m_old, jnp.max(s, axis=-1, keepdims=True))
        a = jnp.exp(m_old - m_new)
        p = jnp.exp(s - m_new)
        l_ref[...] = a * l_ref[...] + jnp.sum(p, axis=-1, keepdims=True)
        m_ref[...] = m_new
        p_ref[:, :width] = p.astype(BF16)
        acc_ref[...] = a * acc_ref[...]

    @pl.when(pg == 0)
    def _():
        m_ref[...] = jnp.full(m_ref.shape, NEG_INF, F32)
        l_ref[...] = jnp.zeros_like(l_ref)
        acc_ref[...] = jnp.zeros_like(acc_ref)
        nk = knew_ref.shape[0]
        t = lax.broadcasted_iota(jnp.int32, (Q_ROWS, nk), 0)
        key = lax.broadcasted_iota(jnp.int32, (Q_ROWS, nk), 1)
        mask = (key >= n * dec_seq) & (key <= n * dec_seq + t)
        for h in range(H_B):
            k_h = knew_ref[:, h * dh:(h + 1) * dh].astype(BF16)
            s = lax.dot_general(q_ref[hrows(h), :], k_h, nt_dims, preferred_element_type=F32)
            s = s * scale + cq_ref[hrows(h), :] - cknew_ref[h:h + 1, :]
            s_ref[hrows(h), 0:nk] = jnp.where(mask, s, NEG_INF)
        softmax_update(nk)
        for h in range(H_B):
            v_h = vnew_ref[:, h * dh:(h + 1) * dh].astype(BF16)
            acc_ref[hrows(h), :] += jnp.dot(p_ref[hrows(h), 0:nk], v_h, preferred_element_type=F32)

    for r in range(pp):
        cols = slice(r * PAGE_SIZE, (r + 1) * PAGE_SIZE)
        for h in range(H_B):
            k_h = kp_refs[r][pl.ds(h, PAGE_SIZE, stride=H_B), :].astype(BF16)
            s = lax.dot_general(q_ref[hrows(h), :], k_h, nt_dims, preferred_element_type=F32)
            s_ref[hrows(h), cols] = s * scale + cq_ref[hrows(h), :] - ckpast_ref[h:h + 1, cols]
    softmax_update(pp * PAGE_SIZE)
    for h in range(H_B):
        pv = None
        for r in range(pp):
            cols = slice(r * PAGE_SIZE, (r + 1) * PAGE_SIZE)
            v_h = vp_refs[r][pl.ds(h, PAGE_SIZE, stride=H_B), :].astype(BF16)
            d = jnp.dot(p_ref[hrows(h), cols], v_h, preferred_element_type=F32)
            pv = d if pv is None else pv + d
        acc_ref[hrows(h), :] += pv

    @pl.when(pg == pl.num_programs(1) - 1)
    def _():
        o_ref[...] = acc_ref[...] / l_ref[...]


def _fox_decode(page_table, q_s, cq_s, k_new, v_new, ck_new, ck_past, cache_k_rows, cache_v_rows, layer, n_pool,
                scale, dec_seq):
    n, n_pages = page_table.shape
    rows, dh = q_s.shape[1:]
    d = k_new.shape[1]
    nk = ck_new.shape[1]
    pp = PAGES_PER_STEP
    steps = n_pages // pp
    page_rows = PAGE_SIZE * H_B

    def page_spec(r):
        return pl.BlockSpec((page_rows, dh), lambda b, p, pt: (layer * n_pool + pt[b, p * pp + r], 0))

    in_specs = [
        pl.BlockSpec((None, rows, dh), lambda b, p, pt: (b, 0, 0)),
        pl.BlockSpec((None, rows, 1), lambda b, p, pt: (b, 0, 0)),
        pl.BlockSpec((nk, d), lambda b, p, pt: (0, 0)),
        pl.BlockSpec((nk, d), lambda b, p, pt: (0, 0)),
        pl.BlockSpec((H_B, nk), lambda b, p, pt: (0, 0)),
        pl.BlockSpec((None, H_B, pp * PAGE_SIZE), lambda b, p, pt: (b, 0, p)),
    ]
    in_specs += [page_spec(r) for r in range(pp)]
    in_specs += [page_spec(r) for r in range(pp)]
    grid_spec = pltpu.PrefetchScalarGridSpec(
        num_scalar_prefetch=1,
        grid=(n, steps),
        in_specs=in_specs,
        out_specs=pl.BlockSpec((None, rows, dh), lambda b, p, pt: (b, 0, 0)),
        scratch_shapes=[pltpu.VMEM((rows, pp * PAGE_SIZE), F32), pltpu.VMEM((rows, pp * PAGE_SIZE), BF16),
                        pltpu.VMEM((rows, 1), F32), pltpu.VMEM((rows, 1), F32), pltpu.VMEM((rows, dh), F32)],
    )
    return pl.pallas_call(
        functools.partial(_fox_decode_kernel, scale=scale, dec_seq=dec_seq),
        grid_spec=grid_spec,
        out_shape=jax.ShapeDtypeStruct((n, rows, dh), F32),
        compiler_params=_cparams("arbitrary", "arbitrary"),
        name="fox_decode",
    )(page_table, q_s, cq_s, k_new, v_new, ck_new, ck_past,
      *([cache_k_rows] * pp), *([cache_v_rows] * pp))


def _router_kernel(x_ref, w_ref, b_ref, comb_ref, mask_ref):
    logits = jnp.dot(x_ref[...], w_ref[...].astype(BF16), preferred_element_type=F32) + b_ref[...]
    ne = logits.shape[1]
    col = lax.broadcasted_iota(jnp.int32, logits.shape, 1)
    m1 = jnp.max(logits, axis=-1, keepdims=True)
    i1 = jnp.min(jnp.where(logits == m1, col, ne), axis=-1, keepdims=True)
    rest = jnp.where(col == i1, -jnp.inf, logits)
    m2 = jnp.max(rest, axis=-1, keepdims=True)
    i2 = jnp.min(jnp.where(rest == m2, col, ne), axis=-1, keepdims=True)
    e2 = jnp.exp(m2 - m1)
    den = 1.0 + e2
    comb_ref[...] = jnp.where(col == i1, 1.0 / den, 0.0) + jnp.where(col == i2, e2 / den, 0.0)
    mask_ref[...] = ((col == i1) | (col == i2)).astype(jnp.int32)


def _router(xb, w_r, b_r, layer):
    mp, d = xb.shape
    ne = w_r.shape[-1]
    return pl.pallas_call(
        _router_kernel,
        grid=(mp // TM,),
        in_specs=[
            pl.BlockSpec((TM, d), lambda i: (i, 0)),
            pl.BlockSpec((None, d, ne), lambda i: (layer, 0, 0)),
            pl.BlockSpec((1, ne), lambda i: (0, 0)),
        ],
        out_specs=[pl.BlockSpec((TM, ne), lambda i: (i, 0)), pl.BlockSpec((TM, ne), lambda i: (i, 0))],
        out_shape=[jax.ShapeDtypeStruct((mp, ne), F32), jax.ShapeDtypeStruct((mp, ne), jnp.int32)],
        compiler_params=_cparams("arbitrary"),
        name="router",
    )(xb, w_r, b_r)


def _row_copy(src, dst, src_row, dst_row, sem):
    return pltpu.make_async_copy(src.at[pl.ds(src_row, 1)], dst.at[pl.ds(dst_row, 1)], sem)


def _dispatch_kernel(pa_ref, pb_ref, x_ref, xs_in_ref, xs_ref, buf_ref, sem_ref):
    del xs_in_ref
    i = pl.program_id(0)
    slot = i % 2
    half = x_ref.shape[1] // 2
    buf = buf_ref.at[slot]
    sem = sem_ref.at[slot]

    def drain(b, s):
        for _ in range(2):
            pltpu.make_async_copy(b, xs_ref.at[pl.ds(0, TM)], s).wait()

    @pl.when(i >= 2)
    def _():
        drain(buf, sem)

    x = x_ref[...]
    hi = x[:, :half].astype(BF16).astype(F32)
    lo = x[:, half:].astype(BF16).astype(F32)
    buf[...] = pltpu.bitcast(hi, U32) | (pltpu.bitcast(lo, U32) >> 16)

    def send(r, c):
        t = i * TM + r
        _row_copy(buf, xs_ref, r, pa_ref[t], sem).start()
        _row_copy(buf, xs_ref, r, pb_ref[t], sem).start()
        return c

    lax.fori_loop(0, TM, send, 0, unroll=DMA_UNROLL)

    @pl.when(i == pl.num_programs(0) - 1)
    def _():
        drain(buf, sem)

        @pl.when(i >= 1)
        def _():
            drain(buf_ref.at[1 - slot], sem_ref.at[1 - slot])


def _dispatch(pos_a, pos_b, x, n_rows):
    mp, d = x.shape
    words = d // 2
    assert words % LANES == 0
    xs0 = jnp.zeros((n_rows, words), U32)
    grid_spec = pltpu.PrefetchScalarGridSpec(
        num_scalar_prefetch=2,
        grid=(mp // TM,),
        in_specs=[pl.BlockSpec((TM, d), lambda i, pa, pb: (i, 0)), pl.BlockSpec(memory_space=pl.ANY)],
        out_specs=pl.BlockSpec(memory_space=pl.ANY),
        scratch_shapes=[pltpu.VMEM((2, TM, words), U32), pltpu.SemaphoreType.DMA((2,))],
    )
    return pl.pallas_call(
        _dispatch_kernel,
        grid_spec=grid_spec,
        out_shape=jax.ShapeDtypeStruct(xs0.shape, U32),
        input_output_aliases={3: 0},
        compiler_params=_cparams("arbitrary"),
        name="moe_dispatch",
    )(pos_a, pos_b, x, xs0)


def _expert_mlp_kernel(te_ref, tv_ref, tb_ref, xs_ref, wg_ref, wu_ref, wd_ref, o_ref, xb_ref):
    i = pl.program_id(0)
    f = pl.program_id(1)
    valid = tv_ref[i]
    tg = o_ref.shape[0]
    sb = tg // N_SUB
    half = o_ref.shape[1] // 2

    @pl.when((f == 0) & (valid > 0))
    def _():
        u = xs_ref[...]
        xb_ref[:, :half] = pltpu.bitcast(u & jnp.uint32(0xFFFF0000), F32).astype(BF16)
        xb_ref[:, half:] = pltpu.bitcast(u << 16, F32).astype(BF16)
        o_ref[...] = jnp.zeros_like(o_ref)

    @pl.when((f == 0) & (valid == 0))
    def _():
        o_ref[...] = jnp.zeros_like(o_ref)

    def mlp(rows):
        x = xb_ref[rows, :]
        g = jnp.dot(x, wg_ref[...].astype(BF16), preferred_element_type=F32)
        u = jnp.dot(x, wu_ref[...].astype(BF16), preferred_element_type=F32)
        h = (jax.nn.silu(g) * u).astype(BF16)
        o_ref[rows, :] += jnp.dot(h, wd_ref[...].astype(BF16), preferred_element_type=F32)

    whole = valid > (N_SUB - 1) * sb

    @pl.when(whole)
    def _():
        mlp(slice(0, tg))

    for s in range(N_SUB - 1):
        @pl.when(jnp.logical_not(whole) & (valid > s * sb))
        def _():
            mlp(slice(s * sb, (s + 1) * sb))


def _expert_mlp(te, tv, tb, xs, w_gate, w_up, w_down, layer, tg):
    n_tiles = te.shape[0]
    d, ff = w_gate.shape[-2:]
    nf = ff // TF
    fcol = lambda i, f, tv: jnp.where(tv[i] > 0, f, nf - 1)
    grid_spec = pltpu.PrefetchScalarGridSpec(
        num_scalar_prefetch=3,
        grid=(n_tiles, nf),
        in_specs=[
            pl.BlockSpec((tg, xs.shape[1]), lambda i, f, te, tv, tb: (tb[i], 0)),
            pl.BlockSpec((None, None, d, TF), lambda i, f, te, tv, tb: (layer, te[i], 0, fcol(i, f, tv))),
            pl.BlockSpec((None, None, d, TF), lambda i, f, te, tv, tb: (layer, te[i], 0, fcol(i, f, tv))),
            pl.BlockSpec((None, None, TF, d), lambda i, f, te, tv, tb: (layer, te[i], fcol(i, f, tv), 0)),
        ],
        out_specs=pl.BlockSpec((tg, d), lambda i, f, te, tv, tb: (i, 0)),
        scratch_shapes=[pltpu.VMEM((tg, d), BF16)],
    )
    return pl.pallas_call(
        _expert_mlp_kernel,
        grid_spec=grid_spec,
        out_shape=jax.ShapeDtypeStruct((n_tiles * tg, d), F32),
        compiler_params=_cparams("arbitrary", "arbitrary"),
        name="expert_mlp",
    )(te, tv, tb, xs, w_gate, w_up, w_down)


def _combine_ln_kernel(pa_ref, pb_ref, r_ref, ga_ref, gb_ref, g_ref, b_ref, y_ref, *rest, alpha, n_head_tiles):
    if n_head_tiles is None:
        oh_ref, ob_ref, ya_ref, yb_ref, sem_ref = rest
        ot_ref = None
    else:
        oh_ref, ot_ref, ob_ref, ya_ref, yb_ref, sem_ref = rest
    i = pl.program_id(0)
    n = pl.num_programs(0)
    slot = i % 2

    def fetch(tile, s):
        def body(r, c):
            t = tile * TC + r
            _row_copy(y_ref, ya_ref.at[s], pa_ref[t], r, sem_ref.at[s]).start()
            _row_copy(y_ref, yb_ref.at[s], pb_ref[t], r, sem_ref.at[s]).start()
            return c
        lax.fori_loop(0, TC, body, 0, unroll=DMA_UNROLL)

    @pl.when(i == 0)
    def _():
        fetch(0, 0)

    @pl.when(i + 1 < n)
    def _():
        fetch(i + 1, 1 - slot)

    pltpu.make_async_copy(y_ref.at[pl.ds(0, TC)], ya_ref.at[slot], sem_ref.at[slot]).wait()
    pltpu.make_async_copy(y_ref.at[pl.ds(0, TC)], yb_ref.at[slot], sem_ref.at[slot]).wait()
    moe = ga_ref[...] * ya_ref[slot] + gb_ref[...] * yb_ref[slot]
    y = _layer_norm_rows(alpha * r_ref[...] + moe, g_ref[...], b_ref[...])
    ob_ref[...] = y.astype(BF16)
    if n_head_tiles is None:
        oh_ref[...] = y
        return

    @pl.when(i < n_head_tiles)
    def _():
        oh_ref[...] = y

    @pl.when(i >= n_head_tiles)
    def _():
        ot_ref[...] = y


def _combine_ln(pos_a, pos_b, resid, gate_a, gate_b, g, b, y, alpha, m_head=None):
    mp, d = resid.shape
    row = lambda i, pa, pb: (i, 0)
    fixed = lambda i, pa, pb: (0, 0)
    if m_head is None:
        n_head_tiles = None
        f32_specs = [pl.BlockSpec((TC, d), row)]
        f32_shapes = [jax.ShapeDtypeStruct((mp, d), F32)]
    else:
        n_head_tiles = m_head // TC
        f32_specs = [pl.BlockSpec((TC, d), lambda i, pa, pb: (jnp.minimum(i, n_head_tiles - 1), 0)),
                     pl.BlockSpec((TC, d), lambda i, pa, pb: (jnp.maximum(i - n_head_tiles, 0), 0))]
        f32_shapes = [jax.ShapeDtypeStruct((m_head, d), F32), jax.ShapeDtypeStruct((mp - m_head, d), F32)]
    grid_spec = pltpu.PrefetchScalarGridSpec(
        num_scalar_prefetch=2,
        grid=(mp // TC,),
        in_specs=[
            pl.BlockSpec((TC, d), row),
            pl.BlockSpec((TC, 1), row),
            pl.BlockSpec((TC, 1), row),
            pl.BlockSpec((1, d), fixed),
            pl.BlockSpec((1, d), fixed),
            pl.BlockSpec(memory_space=pl.ANY),
        ],
        out_specs=f32_specs + [pl.BlockSpec((TC, d), row)],
        scratch_shapes=[pltpu.VMEM((2, TC, d), F32), pltpu.VMEM((2, TC, d), F32), pltpu.SemaphoreType.DMA((2,))],
    )
    outs = pl.pallas_call(
        functools.partial(_combine_ln_kernel, alpha=alpha, n_head_tiles=n_head_tiles),
        grid_spec=grid_spec,
        out_shape=f32_shapes + [jax.ShapeDtypeStruct((mp, d), BF16)],
        compiler_params=_cparams("arbitrary"),
        name="moe_combine_ln",
    )(pos_a, pos_b, resid, gate_a, gate_b, g, b, y)
    if m_head is None:
        return outs[0], outs[1]
    return (outs[0], outs[1]), outs[2]


def _expert_row_tile(m_real, ne):
    target = -(-(TOP_K * m_real * 21) // (ne * 20))
    n_split = max(1, (target + 512) // 1024)
    unit = SUBLANES * 2 * N_SUB
    return -(-target // (n_split * unit)) * unit


def _moe(x, xb, m_real, w_router, b_router, w_gate, w_up, w_down, layer, ln_g, ln_b, alpha, m_head=None):
    mp, d = x.shape
    ne = w_router.shape[-1]
    tg = _expert_row_tile(m_real, ne)
    comb, mask = _router(xb, w_router, b_router, layer)
    live = (jnp.arange(mp, dtype=jnp.int32) < m_real)[:, None]
    mask = jnp.where(live, mask, 0)
    col = jnp.arange(ne, dtype=jnp.int32)[None, :]
    rank = jnp.cumsum(mask, axis=0) - mask
    counts = jnp.sum(mask, axis=0)
    tiles_e = (counts + tg - 1) // tg
    tile_end = jnp.cumsum(tiles_e)
    tile_start = tile_end - tiles_e
    n_tiles = (TOP_K * m_real) // tg + ne
    p = n_tiles * tg
    pos = tile_start[None, :] * tg + rank
    e_hi = jnp.max(jnp.where(mask > 0, col, -1), axis=1, keepdims=True)
    e_lo = jnp.min(jnp.where(mask > 0, col, ne), axis=1, keepdims=True)
    pos_a = jnp.sum(jnp.where(col == e_hi, pos, 0), axis=1)
    pos_b = jnp.sum(jnp.where(col == e_lo, pos, 0), axis=1)
    gate_a = jnp.sum(jnp.where(col == e_hi, comb, 0.0), axis=1, keepdims=True)
    gate_b = jnp.sum(jnp.where(col == e_lo, comb, 0.0), axis=1, keepdims=True)
    spare = p + 2 * (jnp.arange(mp, dtype=jnp.int32) - m_real)
    send_a = jnp.where(live[:, 0], pos_a, spare)
    send_b = jnp.where(live[:, 0], pos_b, spare + 1)
    tile_id = jnp.arange(n_tiles, dtype=jnp.int32)
    n_active = tile_end[-1]
    tile_c = jnp.minimum(tile_id, jnp.maximum(n_active - 1, 0))
    te = jnp.minimum(jnp.sum((tile_end[None, :] <= tile_c[:, None]).astype(jnp.int32), axis=1), ne - 1)
    tv = jnp.clip(counts[te] - (tile_c - tile_start[te]) * tg, 0, tg)
    tv = jnp.where(tile_id < n_active, tv, 0).astype(jnp.int32)
    tb = tile_c.astype(jnp.int32)

    xs = _dispatch(send_a.astype(jnp.int32), send_b.astype(jnp.int32), x, p + 2 * (mp - m_real))
    y = _expert_mlp(te.astype(jnp.int32), tv, tb, xs, w_gate, w_up, w_down, layer, tg)
    return _combine_ln(pos_a.astype(jnp.int32), pos_b.astype(jnp.int32), x, gate_a, gate_b, ln_g, ln_b, y, alpha,
                       m_head)


def kernel(x_prompt, x_sample, cache_k, cache_v, cache_logf, state_conv, page_table, a_w_in, a_ln_g, a_ln_b, a_w_s, a_b_s, a_w_out, b_w_in, b_b_f, b_w_o, c_w_in, c_conv_w, c_w_out, ffn_w_gate, ffn_w_up, ffn_w_down, moe_w_router, moe_b_router, moe_w_gate, moe_w_up, moe_w_down, ln1_g, ln1_b, ln2_g, ln2_b):
    batch, seq, d = x_prompt.shape
    dec_batch, dec_seq, _ = x_sample.shape
    depth = ln1_g.shape[0]
    alpha = (2.0 * depth) ** 0.25
    m_prompt = batch * seq
    m_samp = dec_batch * dec_seq
    m_real = m_prompt + m_samp
    assert seq % TM == 0 and m_samp <= CHUNK and TM % CHUNK == 0 and dec_seq <= SUBLANES
    n_prompt_tiles = m_prompt // TM
    tiles_per_seq = seq // TM
    mp = m_prompt + TM
    n_pool = cache_k.shape[1]
    n_pages = page_table.shape[1]
    assert n_pages % PAGES_PER_STEP == 0 and n_pages % SUFFIX_PAGES_PER_STEP == 0
    dh = d // H_B
    e_a = a_w_out.shape[1]
    e_c = c_w_out.shape[1]
    g_a = e_a // H_A
    assert e_a == d and e_c == d and g_a % LANES == 0 and dh == LANES

    x = jnp.concatenate([x_prompt.reshape(m_prompt, d), x_sample.reshape(m_samp, d),
                         jnp.zeros((mp - m_real, d), F32)], axis=0)
    xb = x.astype(BF16)

    r = jnp.arange(TM)
    tri_prompt = r[:, None] >= r[None, :]
    tri_tail = tri_prompt & ((r[:, None] // dec_seq) == (r[None, :] // dec_seq))
    tri = jnp.stack([tri_prompt, tri_tail]).astype(BF16)
    rp = jnp.arange(PAGE_SIZE)
    tri_upper = (rp[None, :] > rp[:, None]).astype(BF16)

    k_p, v_p, lf_p, k_s, v_s, lf_s, conv_p, conv_s, chunk_v_s = [], [], [], [], [], [], [], [], []
    n_mix = 3
    for li in range(depth):
        kind = li % n_mix
        j = li // n_mix
        g1, b1 = ln1_g[li][None], ln1_b[li][None]
        g2, b2 = ln2_g[li][None], ln2_b[li][None]
        if kind == 0:
            nt = e_a // TN
            u, v = _proj(xb, [(a_w_in, j, 0), (a_w_in, j, nt)], _ep_gelu, [_tile_spec(), _tile_spec()],
                         [jax.ShapeDtypeStruct((mp, e_a), BF16), jax.ShapeDtypeStruct((mp, e_a), F32)],
                         n_col_tiles=nt, name="a_in")
            rc = jnp.arange(CHUNK)
            causal = rc[:, None] >= rc[None, :]
            w_prompt = jnp.where(causal[None], a_w_s[j], 0.0)
            same = (rc[:, None] // dec_seq) == (rc[None, :] // dec_seq)
            t_in = rc % dec_seq
            w_tail = jnp.where((causal & same)[None], a_w_s[j][:, t_in][:, :, t_in], 0.0)
            wmix = jnp.stack([w_prompt, w_tail]).astype(BF16)
            bias_p = jnp.repeat(a_b_s[j].T, g_a, axis=1)
            bias_t = jnp.repeat(a_b_s[j][:, t_in].T, g_a, axis=1)
            bias = jnp.stack([bias_p, bias_t])
            us, vs = _spatial(u, v, a_ln_g[j][None], a_ln_b[j][None], wmix, bias, n_prompt_tiles)
            chunk_v_s.append(vs[:m_samp].reshape(dec_batch, dec_seq, e_a))
            x, xb = _out_ln(us, a_w_out[j].astype(BF16), x, g1, b1, alpha, TM)
        elif kind == 1:
            nt = d // TN
            head_spec = pl.BlockSpec((TM, TN), lambda jj, i: (jnp.minimum(i, n_prompt_tiles - 1), jj))
            tail_spec = pl.BlockSpec((TM, TN), lambda jj, i: (jnp.maximum(i - n_prompt_tiles, 0), jj))
            full16 = jax.ShapeDtypeStruct((mp, d), BF16)
            head32 = jax.ShapeDtypeStruct((m_prompt, d), F32)
            tail32 = jax.ShapeDtypeStruct((mp - m_prompt, d), F32)
            q16, k16, v16, k32p, k32t, v32p, v32t = _proj(
                xb, [(b_w_in, j, 0), (b_w_in, j, nt), (b_w_in, j, 2 * nt)],
                functools.partial(_ep_qkv, n_prompt_tiles=n_prompt_tiles),
                [_tile_spec(), _tile_spec(), _tile_spec(), head_spec, tail_spec, head_spec, tail_spec],
                [full16, full16, full16, head32, tail32, head32, tail32], n_col_tiles=nt, name="b_in_qkv")
            w_f = b_w_in[j][:, 3 * d:]
            lf, c = _logf(xb, w_f, b_b_f[j][None, :], tri, n_prompt_tiles, tiles_per_seq)
            scale = dh ** -0.5
            c_t = c.T
            cq = c_t[:, :m_prompt, None]
            ck = c_t[:, None, :m_prompt]
            o_prompt = _fox_prompt(q16, k16, v16, cq, ck, batch, seq, scale)

            cache_k_rows = cache_k.reshape(-1, dh)
            cache_v_rows = cache_v.reshape(-1, dh)
            cache_lf_rows = cache_logf.reshape(-1, H_B)
            suffix = _suffix(page_table, cache_lf_rows, j, n_pool, tri_upper)
            ck_past = -suffix.reshape(dec_batch, n_pages * PAGE_SIZE, H_B).transpose(0, 2, 1)
            qpad = ((0, 0), (0, 0), (0, Q_ROWS - dec_seq), (0, 0))
            q_s = q16[m_prompt:m_real].reshape(dec_batch, dec_seq, H_B, dh).transpose(0, 2, 1, 3)
            q_s = jnp.pad(q_s, qpad).reshape(dec_batch, H_B * Q_ROWS, dh)
            c_s = c[m_prompt:m_real].reshape(dec_batch, dec_seq, H_B, 1).transpose(0, 2, 1, 3)
            cq_s = jnp.pad(c_s, qpad).reshape(dec_batch, H_B * Q_ROWS, 1)
            o_samp = _fox_decode(page_table, q_s, cq_s, k32t, v32t, c_t[:, m_prompt:m_prompt + CHUNK], ck_past,
                                 cache_k_rows, cache_v_rows, j, n_pool, scale, dec_seq)
            o_samp = o_samp.reshape(dec_batch, H_B, Q_ROWS, dh)[:, :, :dec_seq].transpose(0, 2, 1, 3)
            o_samp = o_samp.reshape(m_samp, d).astype(BF16)
            o_all = jnp.concatenate([o_prompt, o_samp, jnp.zeros((mp - m_real, d), BF16)], axis=0)
            x, xb = _out_ln(o_all, b_w_o[j].astype(BF16), x, g1, b1, alpha, TM)
            k_p.append(k32p.reshape(batch, seq, H_B, dh))
            v_p.append(v32p.reshape(batch, seq, H_B, dh))
            lf_p.append(lf[:m_prompt].reshape(batch, seq, H_B))
            k_s.append(k32t[:m_samp].reshape(dec_batch, dec_seq, H_B, dh))
            v_s.append(v32t[:m_samp].reshape(dec_batch, dec_seq, H_B, dh))
            lf_s.append(lf[m_prompt:m_real].reshape(dec_batch, dec_seq, H_B))
        else:
            nt = e_c // TN
            st = state_conv[j]
            p1 = jnp.zeros((dec_batch, dec_seq, e_c), F32).at[:, 0].set(st[:, 1])
            p2 = jnp.zeros((dec_batch, dec_seq, e_c), F32).at[:, 0].set(st[:, 0]).at[:, 1].set(st[:, 1])
            pad = jnp.zeros((TM - m_samp, e_c), F32)
            p1 = jnp.concatenate([p1.reshape(m_samp, e_c), pad], axis=0)
            p2 = jnp.concatenate([p2.reshape(m_samp, e_c), pad], axis=0)
            n_tiles = mp // TM
            ep = functools.partial(_ep_conv, tiles_per_seq=tiles_per_seq, n_prompt_tiles=n_prompt_tiles,
                                   dec_seq=dec_seq)
            bgy, ztail, zs = _proj(
                xb, [(c_w_in, j, 0), (c_w_in, j, nt), (c_w_in, j, 2 * nt)], ep,
                [_tile_spec(), pl.BlockSpec((SUBLANES, TN), lambda jj, i: (i, jj)),
                 pl.BlockSpec((TM, TN), lambda jj, i: (0, jj))],
                [jax.ShapeDtypeStruct((mp, e_c), BF16), jax.ShapeDtypeStruct((n_tiles * SUBLANES, e_c), F32),
                 jax.ShapeDtypeStruct((TM, e_c), F32)],
                n_col_tiles=nt, name="c_in_conv",
                extra_in=[c_conv_w, p1, p2],
                extra_specs=[pl.BlockSpec((None, CONV_W, TN), lambda jj, i: (j, 0, jj)),
                             pl.BlockSpec((TM, TN), lambda jj, i: (0, jj)),
                             pl.BlockSpec((TM, TN), lambda jj, i: (0, jj))],
                extra_scratch=[pltpu.VMEM((SUBLANES, TN), F32)])
            zt = ztail.reshape(n_tiles, SUBLANES, e_c)
            last_tiles = jnp.arange(batch) * tiles_per_seq + tiles_per_seq - 1
            conv_p.append(zt[last_tiles][:, SUBLANES - (CONV_W - 1):, :])
            conv_s.append(zs[:m_samp].reshape(dec_batch, dec_seq, e_c)[:, dec_seq - (CONV_W - 1):, :])
            x, xb = _out_ln(bgy, c_w_out[j].astype(BF16), x, g1, b1, alpha, TM)

        f = li // 2
        if li % 2 == 0:
            nt = ffn_w_gate.shape[-1] // TN
            (h,) = _proj(xb, [(ffn_w_gate, f, 0), (ffn_w_up, f, 0)], _ep_swiglu, [_tile_spec()],
                         [jax.ShapeDtypeStruct((mp, ffn_w_gate.shape[-1]), BF16)], n_col_tiles=nt, name="ffn_up")
            x, xb = _out_ln(h, ffn_w_down[f].astype(BF16), x, g2, b2, alpha, TM_WIDE_K)
        else:
            last = li == depth - 1
            x, xb = _moe(x, xb, m_real, moe_w_router, moe_b_router[f][None], moe_w_gate, moe_w_up, moe_w_down, f,
                         g2, b2, alpha, m_prompt if last else None)

    if isinstance(x, tuple):
        x_head, x_tail = x
    else:
        x_head, x_tail = x[:m_prompt], x[m_prompt:]
    y_prompt = x_head.reshape(batch, seq, d)
    y_sample = x_tail[:m_samp].reshape(dec_batch, dec_seq, d)
    return (y_prompt, y_sample, jnp.stack(k_p), jnp.stack(v_p), jnp.stack(lf_p), jnp.stack(k_s),
            jnp.stack(v_s), jnp.stack(lf_s), jnp.stack(conv_p), jnp.stack(conv_s), jnp.stack(chunk_v_s))
```

```python
import functools

import jax
import jax.numpy as jnp
from jax import lax
from jax.experimental import pallas as pl
from jax.experimental.pallas import tpu as pltpu

F32 = jnp.float32
BF16 = jnp.bfloat16
U32 = jnp.uint32

H_A = 16
CHUNK = 128
H_B = 16
PAGE_SIZE = 128
CONV_W = 3
TOP_K = 2
LN_EPS = 1e-5
NEG_INF = -1e30

SUBLANES = 8
LANES = 128

TM = 512
TN = 512
TM_WIDE_K = 256
TQ = 512
TKV = 512
HEADS_PER_STEP = 2
PAGES_PER_STEP = 4
SUFFIX_PAGES_PER_STEP = 8
Q_ROWS = 16
TF = 256
N_SUB = 4
TC = 256
DMA_UNROLL = 8

VMEM_LIMIT = 56 * 1024 * 1024


def _cparams(*sem):
    return pltpu.CompilerParams(dimension_semantics=sem, vmem_limit_bytes=VMEM_LIMIT)


def _layer_norm_rows(y, g, b):
    mu = jnp.mean(y, axis=-1, keepdims=True)
    d = y - mu
    var = jnp.mean(d * d, axis=-1, keepdims=True)
    return d * lax.rsqrt(var + LN_EPS) * g + b


def _proj_kernel(*refs, n_w, n_extra, n_out, epilogue):
    x_ref = refs[0]
    w_refs = refs[1:1 + n_w]
    extra = refs[1 + n_w:1 + n_w + n_extra]
    outs = refs[1 + n_w + n_extra:1 + n_w + n_extra + n_out]
    scratch = refs[1 + n_w + n_extra + n_out:]
    wb_refs = scratch[:n_w]
    rest = scratch[n_w:]

    @pl.when(pl.program_id(1) == 0)
    def _():
        for w_ref, wb_ref in zip(w_refs, wb_refs):
            wb_ref[...] = w_ref[...].astype(BF16)

    x = x_ref[...]
    accs = [jnp.dot(x, wb_ref[...], preferred_element_type=F32) for wb_ref in wb_refs]
    epilogue(accs, extra, outs, rest)


def _proj(x, weights, epilogue, out_specs, out_shapes, *, n_col_tiles, name, extra_in=(), extra_specs=(),
          extra_scratch=(), tn=TN):
    mp, k = x.shape
    n_w = len(weights)
    in_specs = [pl.BlockSpec((TM, k), lambda j, i: (i, 0))]
    args = [x]
    for w, layer, off in weights:
        in_specs.append(pl.BlockSpec((None, k, tn), functools.partial(
            lambda j, i, layer, off: (layer, 0, j + off), layer=layer, off=off)))
        args.append(w)
    in_specs += list(extra_specs)
    args += list(extra_in)
    kern = functools.partial(_proj_kernel, n_w=n_w, n_extra=len(extra_in), n_out=len(out_shapes),
                             epilogue=epilogue)
    return pl.pallas_call(
        kern,
        grid=(n_col_tiles, mp // TM),
        in_specs=in_specs,
        out_specs=out_specs,
        out_shape=out_shapes,
        scratch_shapes=[pltpu.VMEM((k, tn), BF16) for _ in range(n_w)] + list(extra_scratch),
        compiler_params=_cparams("arbitrary", "arbitrary"),
        name=name,
    )(*args)


def _tile_spec(tn=TN):
    return pl.BlockSpec((TM, tn), lambda j, i: (i, j))


def _ep_gelu(accs, extra, outs, rest):
    for acc, o in zip(accs, outs):
        o[...] = jax.nn.gelu(acc).astype(o.dtype)


def _ep_qkv(accs, extra, outs, rest, *, n_prompt_tiles):
    q16, k16, v16, k32p, k32t, v32p, v32t = outs
    q16[...] = accs[0].astype(BF16)
    k16[...] = accs[1].astype(BF16)
    v16[...] = accs[2].astype(BF16)
    i = pl.program_id(1)

    @pl.when(i < n_prompt_tiles)
    def _():
        k32p[...] = accs[1]
        v32p[...] = accs[2]

    @pl.when(i >= n_prompt_tiles)
    def _():
        k32t[...] = accs[1]
        v32t[...] = accs[2]


def _ep_swiglu(accs, extra, outs, rest):
    outs[0][...] = (jax.nn.silu(accs[0]) * accs[1]).astype(outs[0].dtype)


def _ep_conv(accs, extra, outs, rest, *, tiles_per_seq, n_prompt_tiles, dec_seq):
    cw_ref, p1_ref, p2_ref = extra
    y_ref, ztail_ref, zs_ref = outs
    (carry_ref,) = rest
    i = pl.program_id(1)
    z = accs[1] * accs[2]
    row = lax.broadcasted_iota(jnp.int32, z.shape, 0)
    r1 = pltpu.roll(z, 1, 0)
    r2 = pltpu.roll(z, 2, 0)
    w0 = cw_ref[0:1, :]
    w1 = cw_ref[1:2, :]
    w2 = cw_ref[2:3, :]

    @pl.when(i % tiles_per_seq == 0)
    def _():
        carry_ref[...] = jnp.zeros_like(carry_ref)

    @pl.when(i < n_prompt_tiles)
    def _():
        c6 = carry_ref[6:7, :]
        c7 = carry_ref[7:8, :]
        s1 = jnp.where(row == 0, c7, r1)
        s2 = jnp.where(row == 0, c6, jnp.where(row == 1, c7, r2))
        y = w0 * s2 + w1 * s1 + w2 * z
        y_ref[...] = (accs[0] * y).astype(y_ref.dtype)

    @pl.when(i >= n_prompt_tiles)
    def _():
        t = row % dec_seq
        s1 = jnp.where(t >= 1, r1, p1_ref[...])
        s2 = jnp.where(t >= 2, r2, p2_ref[...])
        y = w0 * s2 + w1 * s1 + w2 * z
        y_ref[...] = (accs[0] * y).astype(y_ref.dtype)
        zs_ref[...] = z

    carry_ref[...] = z[TM - SUBLANES:, :]
    ztail_ref[...] = z[TM - SUBLANES:, :]


def _out_ln_kernel(x_ref, w_ref, r_ref, g_ref, b_ref, o_ref, ob_ref, *, alpha):
    acc = jnp.dot(x_ref[...], w_ref[...], preferred_element_type=F32)
    y = _layer_norm_rows(alpha * r_ref[...] + acc, g_ref[...], b_ref[...])
    o_ref[...] = y
    ob_ref[...] = y.astype(BF16)


def _out_ln(x, w, layer, resid, g, b, alpha, tm):
    mp, k = x.shape
    d = resid.shape[1]
    row = lambda i: (i, 0)
    fixed = lambda i: (0, 0)
    return pl.pallas_call(
        functools.partial(_out_ln_kernel, alpha=alpha),
        grid=(mp // tm,),
        in_specs=[
            pl.BlockSpec((tm, k), row),
            pl.BlockSpec((None, k, d), lambda i: (layer, 0, 0), pipeline_mode=pl.Buffered(1)),
            pl.BlockSpec((tm, d), row),
            pl.BlockSpec((1, d), fixed),
            pl.BlockSpec((1, d), fixed),
        ],
        out_specs=[pl.BlockSpec((tm, d), row), pl.BlockSpec((tm, d), row)],
        out_shape=[jax.ShapeDtypeStruct((mp, d), F32), jax.ShapeDtypeStruct((mp, d), BF16)],
        compiler_params=_cparams("arbitrary"),
        name="out_ln",
    )(x, w, resid, g, b)


def _spatial_kernel(u_ref, v_ref, g_ref, b_ref, wmix_ref, bias_ref, o_ref, vs_ref, vln_ref, *, n_prompt_tiles):
    i = pl.program_id(0)
    vln = _layer_norm_rows(v_ref[...], g_ref[...], b_ref[...])
    vln_ref[...] = vln.astype(BF16)

    @pl.when(i >= n_prompt_tiles)
    def _():
        vs_ref[...] = vln

    g_a = u_ref.shape[1] // H_A

    def chunk_body(c, carry):
        r0 = pl.multiple_of(c * CHUNK, CHUNK)
        for h in range(H_A):
            cols = slice(h * g_a, (h + 1) * g_a)
            s = jnp.dot(wmix_ref[h], vln_ref[pl.ds(r0, CHUNK), cols], preferred_element_type=F32)
            s = s + bias_ref[:, cols]
            o_ref[pl.ds(r0, CHUNK), cols] = (u_ref[pl.ds(r0, CHUNK), cols].astype(F32) * s).astype(o_ref.dtype)
        return carry

    lax.fori_loop(0, TM // CHUNK, chunk_body, 0)


def _spatial(u, v, ln_g, ln_b, wmix, bias, n_prompt_tiles):
    mp, e = u.shape
    sel = lambda i: jnp.where(i < n_prompt_tiles, 0, 1)
    return pl.pallas_call(
        functools.partial(_spatial_kernel, n_prompt_tiles=n_prompt_tiles),
        grid=(mp // TM,),
        in_specs=[
            pl.BlockSpec((TM, e), lambda i: (i, 0)),
            pl.BlockSpec((TM, e), lambda i: (i, 0)),
            pl.BlockSpec((1, e), lambda i: (0, 0)),
            pl.BlockSpec((1, e), lambda i: (0, 0)),
            pl.BlockSpec((None, H_A, CHUNK, CHUNK), lambda i: (sel(i), 0, 0, 0)),
            pl.BlockSpec((None, CHUNK, e), lambda i: (sel(i), 0, 0)),
        ],
        out_specs=[pl.BlockSpec((TM, e), lambda i: (i, 0)), pl.BlockSpec((TM, e), lambda i: (0, 0))],
        out_shape=[jax.ShapeDtypeStruct((mp, e), BF16), jax.ShapeDtypeStruct((TM, e), F32)],
        scratch_shapes=[pltpu.VMEM((TM, e), BF16)],
        compiler_params=_cparams("arbitrary"),
        name="spatial_mix",
    )(u, v, ln_g, ln_b, wmix, bias)


def _split3(x):
    hi = x.astype(BF16)
    r1 = x - hi.astype(F32)
    mid = r1.astype(BF16)
    lo = (r1 - mid.astype(F32)).astype(BF16)
    return hi, mid, lo


def _tri_matmul(tri, x):
    hi, mid, lo = _split3(x)
    out = jnp.dot(tri, lo, preferred_element_type=F32)
    out = out + jnp.dot(tri, mid, preferred_element_type=F32)
    return out + jnp.dot(tri, hi, preferred_element_type=F32)


def _logf_kernel(x_ref, w_ref, bf_ref, tri_ref, lf_ref, c_ref, cb_ref, carry_ref, *, tiles_per_seq):
    i = pl.program_id(0)

    @pl.when(i % tiles_per_seq == 0)
    def _():
        carry_ref[...] = jnp.zeros_like(carry_ref)

    z = jnp.dot(x_ref[...], w_ref[...].astype(BF16), preferred_element_type=F32) + bf_ref[...]
    lf = jax.nn.log_sigmoid(z)
    lf_ref[...] = lf
    c = _tri_matmul(tri_ref[...], lf) + carry_ref[0:1, :]
    c_ref[...] = c
    carry_ref[0:1, :] = c[TM - 1:TM, :]
    for h in range(c.shape[1]):
        cb_ref[h] = jnp.broadcast_to(c[:, h:h + 1], (TM, LANES))


def _logf(xb, w_f, b_f, tri, n_prompt_tiles, tiles_per_seq):
    mp, d = xb.shape
    h = w_f.shape[1]
    sel = lambda i: jnp.where(i < n_prompt_tiles, 0, 1)
    return pl.pallas_call(
        functools.partial(_logf_kernel, tiles_per_seq=tiles_per_seq),
        grid=(mp // TM,),
        in_specs=[
            pl.BlockSpec((TM, d), lambda i: (i, 0)),
            pl.BlockSpec((d, h), lambda i: (0, 0)),
            pl.BlockSpec((1, h), lambda i: (0, 0)),
            pl.BlockSpec((None, TM, TM), lambda i: (sel(i), 0, 0)),
        ],
        out_specs=[pl.BlockSpec((TM, h), lambda i: (i, 0)), pl.BlockSpec((TM, h), lambda i: (i, 0)),
                   pl.BlockSpec((h, TM, LANES), lambda i: (0, i, 0))],
        out_shape=[jax.ShapeDtypeStruct((mp, h), F32), jax.ShapeDtypeStruct((mp, h), F32),
                   jax.ShapeDtypeStruct((h, mp, LANES), F32)],
        scratch_shapes=[pltpu.VMEM((SUBLANES, h), F32)],
        compiler_params=_cparams("arbitrary"),
        name="logf_cumsum",
    )(xb, w_f, b_f, tri)


def _fox_prompt_kernel(q_ref, k_ref, vt_ref, cq_ref, ckb_ref, o_ref, *, scale):
    qi = pl.program_id(2)
    tq = q_ref.shape[0]
    dh = q_ref.shape[1] // HEADS_PER_STEP
    heads = range(HEADS_PER_STEP)
    qs = [q_ref[:, g * dh:(g + 1) * dh] for g in heads]
    cqs = [cq_ref[g] for g in heads]

    def step(g, kv0, carry, masked):
        m, l, acc = carry
        k = k_ref[pl.ds(kv0, TKV), g * dh:(g + 1) * dh]
        s = lax.dot_general(k, qs[g], (((1,), (1,)), ((), ())), preferred_element_type=F32)
        ckb = ckb_ref[g, pl.ds(kv0, TKV), :]
        s = s * scale + cqs[g] - jnp.concatenate([ckb] * (tq // LANES), axis=1)
        if masked:
            key = lax.broadcasted_iota(jnp.int32, s.shape, 0)
            qrow = lax.broadcasted_iota(jnp.int32, s.shape, 1)
            s = jnp.where(qrow >= key, s, NEG_INF)
        m_new = jnp.maximum(m, jnp.max(s, axis=0, keepdims=True))
        a = jnp.exp(m - m_new)
        p = jnp.exp(s - m_new)
        l = a * l + jnp.sum(p, axis=0, keepdims=True)
        vt = vt_ref[g * dh:(g + 1) * dh, pl.ds(kv0, TKV)]
        acc = a * acc + jnp.dot(vt, p.astype(BF16), preferred_element_type=F32)
        return m_new, l, acc

    init = tuple((jnp.full((1, tq), NEG_INF, F32), jnp.zeros((1, tq), F32), jnp.zeros((dh, tq), F32))
                 for _ in heads)

    def body(j, carries):
        kv0 = pl.multiple_of(j * TKV, TKV)
        return tuple(step(g, kv0, carries[g], False) for g in heads)

    carries = lax.fori_loop(0, qi * (tq // TKV), body, init)
    for g in heads:
        m, l, acc = step(g, pl.multiple_of(qi * tq, tq), carries[g], True)
        o_ref[:, g * dh:(g + 1) * dh] = (acc / l).T.astype(o_ref.dtype)


def _fox_prompt(q, k, vt, cq, ckb, batch, seq, scale):
    dh = q.shape[1] // H_B
    nq = seq // TQ
    w = HEADS_PER_STEP * dh
    return pl.pallas_call(
        functools.partial(_fox_prompt_kernel, scale=scale),
        grid=(batch, H_B // HEADS_PER_STEP, nq),
        in_specs=[
            pl.BlockSpec((TQ, w), lambda b, h, i: (b * nq + i, h)),
            pl.BlockSpec((seq, w), lambda b, h, i: (b, h)),
            pl.BlockSpec((w, seq), lambda b, h, i: (h, b)),
            pl.BlockSpec((HEADS_PER_STEP, 1, TQ), lambda b, h, i: (h, 0, b * nq + i)),
            pl.BlockSpec((HEADS_PER_STEP, seq, LANES), lambda b, h, i: (h, b, 0)),
        ],
        out_specs=pl.BlockSpec((TQ, w), lambda b, h, i: (b * nq + i, h)),
        out_shape=jax.ShapeDtypeStruct((batch * seq, H_B * dh), BF16),
        compiler_params=_cparams("arbitrary", "arbitrary", "arbitrary"),
        name="fox_prompt",
    )(q, k, vt, cq, ckb)


def _suffix_kernel(pt_ref, *refs):
    sp = SUFFIX_PAGES_PER_STEP
    lf_refs = refs[:sp]
    tri_ref, o_ref, carry_ref = refs[sp:]

    @pl.when(pl.program_id(1) == 0)
    def _():
        carry_ref[...] = jnp.zeros_like(carry_ref)

    carry = carry_ref[...]
    tri = tri_ref[...]
    ones = jnp.ones((PAGE_SIZE, LANES), BF16)
    for r in reversed(range(sp)):
        within = jnp.zeros(carry.shape, F32)
        total = jnp.zeros(carry.shape, F32)
        for part in reversed(_split3(lf_refs[r][...])):
            within = within + lax.dot_general(part, tri, (((0,), (1,)), ((), ())), preferred_element_type=F32)
            total = total + lax.dot_general(part, ones, (((0,), (0,)), ((), ())), preferred_element_type=F32)
        o_ref[:, r * PAGE_SIZE:(r + 1) * PAGE_SIZE] = -(within + carry)
        carry = carry + total
    carry_ref[...] = carry


def _suffix(page_table, cache_lf_rows, layer, n_pool, tri_upper):
    n, n_pages = page_table.shape
    h = cache_lf_rows.shape[-1]
    sp = SUFFIX_PAGES_PER_STEP
    groups = n_pages // sp

    def page_spec(r):
        return pl.BlockSpec((PAGE_SIZE, h),
                            lambda b, p, pt: (layer * n_pool + pt[b, (groups - 1 - p) * sp + r], 0))

    grid_spec = pltpu.PrefetchScalarGridSpec(
        num_scalar_prefetch=1,
        grid=(n, groups),
        in_specs=[page_spec(r) for r in range(sp)] + [
            pl.BlockSpec((PAGE_SIZE, PAGE_SIZE), lambda b, p, pt: (0, 0))],
        out_specs=pl.BlockSpec((None, h, sp * PAGE_SIZE), lambda b, p, pt: (b, 0, groups - 1 - p)),
        scratch_shapes=[pltpu.VMEM((h, LANES), F32)],
    )
    return pl.pallas_call(
        _suffix_kernel,
        grid_spec=grid_spec,
        out_shape=jax.ShapeDtypeStruct((n, h, n_pages * PAGE_SIZE), F32),
        compiler_params=_cparams("arbitrary", "arbitrary"),
        name="logf_suffix",
    )(page_table, *([cache_lf_rows] * sp), tri_upper)


def _fox_decode_kernel(pt_ref, q_ref, cq_ref, knew_ref, vnew_ref, cknew_ref, ckpast_ref, *rest, scale, dec_seq):
    pp = PAGES_PER_STEP
    kp_refs = rest[:pp]
    vp_refs = rest[pp:2 * pp]
    o_ref = rest[2 * pp]
    s_ref, p_ref, m_ref, l_ref, acc_ref = rest[2 * pp + 1:]
    n = pl.program_id(0)
    pg = pl.program_id(1)
    dh = q_ref.shape[1]
    nt_dims = (((1,), (1,)), ((), ()))

    def hrows(h):
        return slice(h * Q_ROWS, (h + 1) * Q_ROWS)

    def softmax_update(width):
        s = s_ref[:, :width]
        m_old = m_ref[...]
        m_new = jnp.maximum(m_old, jnp.max(s, axis=-1, keepdims=True))
        a = jnp.exp(m_old - m_new)
        p = jnp.exp(s - m_new)
        l_ref[...] = a * l_ref[...] + jnp.sum(p, axis=-1, keepdims=True)
        m_ref[...] = m_new
        p_ref[:, :width] = p.astype(BF16)
        acc_ref[...] = a * acc_ref[...]

    @pl.when(pg == 0)
    def _():
        m_ref[...] = jnp.full(m_ref.shape, NEG_INF, F32)
        l_ref[...] = jnp.zeros_like(l_ref)
        acc_ref[...] = jnp.zeros_like(acc_ref)
        nk = knew_ref.shape[0]
        t = lax.broadcasted_iota(jnp.int32, (Q_ROWS, nk), 0)
        key = lax.broadcasted_iota(jnp.int32, (Q_ROWS, nk), 1)
        mask = (key >= n * dec_seq) & (key <= n * dec_seq + t)
        for h in range(H_B):
            k_h = knew_ref[:, h * dh:(h + 1) * dh].astype(BF16)
            s = lax.dot_general(q_ref[hrows(h), :], k_h, nt_dims, preferred_element_type=F32)
            s = s * scale + cq_ref[hrows(h), :] - cknew_ref[h:h + 1, :]
            s_ref[hrows(h), 0:nk] = jnp.where(mask, s, NEG_INF)
        softmax_update(nk)
        for h in range(H_B):
            v_h = vnew_ref[:, h * dh:(h + 1) * dh].astype(BF16)
            acc_ref[hrows(h), :] += jnp.dot(p_ref[hrows(h), 0:nk], v_h, preferred_element_type=F32)

    for r in range(pp):
        cols = slice(r * PAGE_SIZE, (r + 1) * PAGE_SIZE)
        for h in range(H_B):
            k_h = kp_refs[r][pl.ds(h, PAGE_SIZE, stride=H_B), :].astype(BF16)
            s = lax.dot_general(q_ref[hrows(h), :], k_h, nt_dims, preferred_element_type=F32)
            s_ref[hrows(h), cols] = s * scale + cq_ref[hrows(h), :] - ckpast_ref[h:h + 1, cols]
    softmax_update(pp * PAGE_SIZE)
    for h in range(H_B):
        pv = None
        for r in range(pp):
            cols = slice(r * PAGE_SIZE, (r + 1) * PAGE_SIZE)
            v_h = vp_refs[r][pl.ds(h, PAGE_SIZE, stride=H_B), :].astype(BF16)
            d = jnp.dot(p_ref[hrows(h), cols], v_h, preferred_element_type=F32)
            pv = d if pv is None else pv + d
        acc_ref[hrows(h), :] += pv

    @pl.when(pg == pl.num_programs(1) - 1)
    def _():
        o_ref[...] = acc_ref[...] / l_ref[...]


def _fox_decode(page_table, q_s, cq_s, k_new, v_new, ck_new, ck_past, cache_k_rows, cache_v_rows, layer, n_pool,
                scale, dec_seq):
    n, n_pages = page_table.shape
    rows, dh = q_s.shape[1:]
    d = k_new.shape[1]
    nk = ck_new.shape[1]
    pp = PAGES_PER_STEP
    steps = n_pages // pp
    page_rows = PAGE_SIZE * H_B

    def page_spec(r):
        return pl.BlockSpec((page_rows, dh), lambda b, p, pt: (layer * n_pool + pt[b, p * pp + r], 0))

    in_specs = [
        pl.BlockSpec((None, rows, dh), lambda b, p, pt: (b, 0, 0)),
        pl.BlockSpec((None, rows, 1), lambda b, p, pt: (b, 0, 0)),
        pl.BlockSpec((nk, d), lambda b, p, pt: (0, 0)),
        pl.BlockSpec((nk, d), lambda b, p, pt: (0, 0)),
        pl.BlockSpec((H_B, nk), lambda b, p, pt: (0, 0)),
        pl.BlockSpec((None, H_B, pp * PAGE_SIZE), lambda b, p, pt: (b, 0, p)),
    ]
    in_specs += [page_spec(r) for r in range(pp)]
    in_specs += [page_spec(r) for r in range(pp)]
    grid_spec = pltpu.PrefetchScalarGridSpec(
        num_scalar_prefetch=1,
        grid=(n, steps),
        in_specs=in_specs,
        out_specs=pl.BlockSpec((None, rows, dh), lambda b, p, pt: (b, 0, 0)),
        scratch_shapes=[pltpu.VMEM((rows, pp * PAGE_SIZE), F32), pltpu.VMEM((rows, pp * PAGE_SIZE), BF16),
                        pltpu.VMEM((rows, 1), F32), pltpu.VMEM((rows, 1), F32), pltpu.VMEM((rows, dh), F32)],
    )
    return pl.pallas_call(
        functools.partial(_fox_decode_kernel, scale=scale, dec_seq=dec_seq),
        grid_spec=grid_spec,
        out_shape=jax.ShapeDtypeStruct((n, rows, dh), F32),
        compiler_params=_cparams("arbitrary", "arbitrary"),
        name="fox_decode",
    )(page_table, q_s, cq_s, k_new, v_new, ck_new, ck_past,
      *([cache_k_rows] * pp), *([cache_v_rows] * pp))


def _router_kernel(x_ref, w_ref, b_ref, comb_ref, mask_ref):
    logits = jnp.dot(x_ref[...], w_ref[...].astype(BF16), preferred_element_type=F32) + b_ref[...]
    ne = logits.shape[1]
    col = lax.broadcasted_iota(jnp.int32, logits.shape, 1)
    m1 = jnp.max(logits, axis=-1, keepdims=True)
    i1 = jnp.min(jnp.where(logits == m1, col, ne), axis=-1, keepdims=True)
    rest = jnp.where(col == i1, -jnp.inf, logits)
    m2 = jnp.max(rest, axis=-1, keepdims=True)
    i2 = jnp.min(jnp.where(rest == m2, col, ne), axis=-1, keepdims=True)
    e2 = jnp.exp(m2 - m1)
    den = 1.0 + e2
    comb_ref[...] = jnp.where(col == i1, 1.0 / den, 0.0) + jnp.where(col == i2, e2 / den, 0.0)
    mask_ref[...] = ((col == i1) | (col == i2)).astype(jnp.int32)


def _router(xb, w_r, b_r, layer):
    mp, d = xb.shape
    ne = w_r.shape[-1]
    return pl.pallas_call(
        _router_kernel,
        grid=(mp // TM,),
        in_specs=[
            pl.BlockSpec((TM, d), lambda i: (i, 0)),
            pl.BlockSpec((None, d, ne), lambda i: (layer, 0, 0)),
            pl.BlockSpec((1, ne), lambda i: (0, 0)),
        ],
        out_specs=[pl.BlockSpec((TM, ne), lambda i: (i, 0)), pl.BlockSpec((TM, ne), lambda i: (i, 0))],
        out_shape=[jax.ShapeDtypeStruct((mp, ne), F32), jax.ShapeDtypeStruct((mp, ne), jnp.int32)],
        compiler_params=_cparams("arbitrary"),
        name="router",
    )(xb, w_r, b_r)


def _row_copy(src, dst, src_row, dst_row, sem):
    return pltpu.make_async_copy(src.at[pl.ds(src_row, 1)], dst.at[pl.ds(dst_row, 1)], sem)


def _dispatch_kernel(pa_ref, pb_ref, x_ref, xs_in_ref, xs_ref, buf_ref, sem_ref):
    del xs_in_ref
    i = pl.program_id(0)
    slot = i % 2
    half = x_ref.shape[1] // 2
    buf = buf_ref.at[slot]
    sem = sem_ref.at[slot]

    def drain(b, s):
        for _ in range(2):
            pltpu.make_async_copy(b, xs_ref.at[pl.ds(0, TM)], s).wait()

    @pl.when(i >= 2)
    def _():
        drain(buf, sem)

    x = x_ref[...]
    hi = x[:, :half].astype(BF16).astype(F32)
    lo = x[:, half:].astype(BF16).astype(F32)
    buf[...] = pltpu.bitcast(hi, U32) | (pltpu.bitcast(lo, U32) >> 16)

    def send(r, c):
        t = i * TM + r
        _row_copy(buf, xs_ref, r, pa_ref[t], sem).start()
        _row_copy(buf, xs_ref, r, pb_ref[t], sem).start()
        return c

    lax.fori_loop(0, TM, send, 0, unroll=DMA_UNROLL)

    @pl.when(i == pl.num_programs(0) - 1)
    def _():
        drain(buf, sem)

        @pl.when(i >= 1)
        def _():
            drain(buf_ref.at[1 - slot], sem_ref.at[1 - slot])


def _dispatch(pos_a, pos_b, x, n_rows):
    mp, d = x.shape
    words = d // 2
    assert words % LANES == 0
    xs0 = jnp.zeros((n_rows, words), U32)
    grid_spec = pltpu.PrefetchScalarGridSpec(
        num_scalar_prefetch=2,
        grid=(mp // TM,),
        in_specs=[pl.BlockSpec((TM, d), lambda i, pa, pb: (i, 0)), pl.BlockSpec(memory_space=pl.ANY)],
        out_specs=pl.BlockSpec(memory_space=pl.ANY),
        scratch_shapes=[pltpu.VMEM((2, TM, words), U32), pltpu.SemaphoreType.DMA((2,))],
    )
    return pl.pallas_call(
        _dispatch_kernel,
        grid_spec=grid_spec,
        out_shape=jax.ShapeDtypeStruct(xs0.shape, U32),
        input_output_aliases={3: 0},
        compiler_params=_cparams("arbitrary"),
        name="moe_dispatch",
    )(pos_a, pos_b, x, xs0)


def _expert_mlp_kernel(te_ref, tv_ref, tb_ref, xs_ref, wg_ref, wu_ref, wd_ref, o_ref, xb_ref):
    i = pl.program_id(0)
    f = pl.program_id(1)
    valid = tv_ref[i]
    tg = o_ref.shape[0]
    sb = tg // N_SUB
    half = o_ref.shape[1] // 2

    @pl.when((f == 0) & (valid > 0))
    def _():
        u = xs_ref[...]
        xb_ref[:, :half] = pltpu.bitcast(u & jnp.uint32(0xFFFF0000), F32).astype(BF16)
        xb_ref[:, half:] = pltpu.bitcast(u << 16, F32).astype(BF16)
        o_ref[...] = jnp.zeros_like(o_ref)

    @pl.when((f == 0) & (valid == 0))
    def _():
        o_ref[...] = jnp.zeros_like(o_ref)

    def mlp(rows):
        x = xb_ref[rows, :]
        g = jnp.dot(x, wg_ref[...].astype(BF16), preferred_element_type=F32)
        u = jnp.dot(x, wu_ref[...].astype(BF16), preferred_element_type=F32)
        h = (jax.nn.silu(g) * u).astype(BF16)
        o_ref[rows, :] += jnp.dot(h, wd_ref[...].astype(BF16), preferred_element_type=F32)

    whole = valid > (N_SUB - 1) * sb

    @pl.when(whole)
    def _():
        mlp(slice(0, tg))

    for s in range(N_SUB - 1):
        @pl.when(jnp.logical_not(whole) & (valid > s * sb))
        def _():
            mlp(slice(s * sb, (s + 1) * sb))


def _expert_mlp(te, tv, tb, xs, w_gate, w_up, w_down, layer, tg):
    n_tiles = te.shape[0]
    d, ff = w_gate.shape[-2:]
    nf = ff // TF
    fcol = lambda i, f, tv: jnp.where(tv[i] > 0, f, nf - 1)
    grid_spec = pltpu.PrefetchScalarGridSpec(
        num_scalar_prefetch=3,
        grid=(n_tiles, nf),
        in_specs=[
            pl.BlockSpec((tg, xs.shape[1]), lambda i, f, te, tv, tb: (tb[i], 0)),
            pl.BlockSpec((None, None, d, TF), lambda i, f, te, tv, tb: (layer, te[i], 0, fcol(i, f, tv))),
            pl.BlockSpec((None, None, d, TF), lambda i, f, te, tv, tb: (layer, te[i], 0, fcol(i, f, tv))),
            pl.BlockSpec((None, None, TF, d), lambda i, f, te, tv, tb: (layer, te[i], fcol(i, f, tv), 0)),
        ],
        out_specs=pl.BlockSpec((tg, d), lambda i, f, te, tv, tb: (i, 0)),
        scratch_shapes=[pltpu.VMEM((tg, d), BF16)],
    )
    return pl.pallas_call(
        _expert_mlp_kernel,
        grid_spec=grid_spec,
        out_shape=jax.ShapeDtypeStruct((n_tiles * tg, d), F32),
        compiler_params=_cparams("arbitrary", "arbitrary"),
        name="expert_mlp",
    )(te, tv, tb, xs, w_gate, w_up, w_down)


def _combine_ln_kernel(pa_ref, pb_ref, r_ref, ga_ref, gb_ref, g_ref, b_ref, y_ref, *rest, alpha, n_head_tiles):
    if n_head_tiles is None:
        oh_ref, ob_ref, ya_ref, yb_ref, sem_ref = rest
        ot_ref = None
    else:
        oh_ref, ot_ref, ob_ref, ya_ref, yb_ref, sem_ref = rest
    i = pl.program_id(0)
    n = pl.num_programs(0)
    slot = i % 2

    def fetch(tile, s):
        def body(r, c):
            t = tile * TC + r
            _row_copy(y_ref, ya_ref.at[s], pa_ref[t], r, sem_ref.at[s]).start()
            _row_copy(y_ref, yb_ref.at[s], pb_ref[t], r, sem_ref.at[s]).start()
            return c
        lax.fori_loop(0, TC, body, 0, unroll=DMA_UNROLL)

    @pl.when(i == 0)
    def _():
        fetch(0, 0)

    @pl.when(i + 1 < n)
    def _():
        fetch(i + 1, 1 - slot)

    pltpu.make_async_copy(y_ref.at[pl.ds(0, TC)], ya_ref.at[slot], sem_ref.at[slot]).wait()
    pltpu.make_async_copy(y_ref.at[pl.ds(0, TC)], yb_ref.at[slot], sem_ref.at[slot]).wait()
    moe = ga_ref[...] * ya_ref[slot] + gb_ref[...] * yb_ref[slot]
    y = _layer_norm_rows(alpha * r_ref[...] + moe, g_ref[...], b_ref[...])
    ob_ref[...] = y.astype(BF16)
    if n_head_tiles is None:
        oh_ref[...] = y
        return

    @pl.when(i < n_head_tiles)
    def _():
        oh_ref[...] = y

    @pl.when(i >= n_head_tiles)
    def _():
        ot_ref[...] = y


def _combine_ln(pos_a, pos_b, resid, gate_a, gate_b, g, b, y, alpha, m_head=None):
    mp, d = resid.shape
    row = lambda i, pa, pb: (i, 0)
    fixed = lambda i, pa, pb: (0, 0)
    if m_head is None:
        n_head_tiles = None
        f32_specs = [pl.BlockSpec((TC, d), row)]
        f32_shapes = [jax.ShapeDtypeStruct((mp, d), F32)]
    else:
        n_head_tiles = m_head // TC
        f32_specs = [pl.BlockSpec((TC, d), lambda i, pa, pb: (jnp.minimum(i, n_head_tiles - 1), 0)),
                     pl.BlockSpec((TC, d), lambda i, pa, pb: (jnp.maximum(i - n_head_tiles, 0), 0))]
        f32_shapes = [jax.ShapeDtypeStruct((m_head, d), F32), jax.ShapeDtypeStruct((mp - m_head, d), F32)]
    grid_spec = pltpu.PrefetchScalarGridSpec(
        num_scalar_prefetch=2,
        grid=(mp // TC,),
        in_specs=[
            pl.BlockSpec((TC, d), row),
            pl.BlockSpec((TC, 1), row),
            pl.BlockSpec((TC, 1), row),
            pl.BlockSpec((1, d), fixed),
            pl.BlockSpec((1, d), fixed),
            pl.BlockSpec(memory_space=pl.ANY),
        ],
        out_specs=f32_specs + [pl.BlockSpec((TC, d), row)],
        scratch_shapes=[pltpu.VMEM((2, TC, d), F32), pltpu.VMEM((2, TC, d), F32), pltpu.SemaphoreType.DMA((2,))],
    )
    outs = pl.pallas_call(
        functools.partial(_combine_ln_kernel, alpha=alpha, n_head_tiles=n_head_tiles),
        grid_spec=grid_spec,
        out_shape=f32_shapes + [jax.ShapeDtypeStruct((mp, d), BF16)],
        compiler_params=_cparams("arbitrary"),
        name="moe_combine_ln",
    )(pos_a, pos_b, resid, gate_a, gate_b, g, b, y)
    if m_head is None:
        return outs[0], outs[1]
    return (outs[0], outs[1]), outs[2]


def _expert_row_tile(m_real, ne):
    target = -(-(TOP_K * m_real * 21) // (ne * 20))
    n_split = max(1, (target + 512) // 1024)
    unit = SUBLANES * 2 * N_SUB
    return -(-target // (n_split * unit)) * unit


def _moe(x, xb, m_real, w_router, b_router, w_gate, w_up, w_down, layer, ln_g, ln_b, alpha, m_head=None):
    mp, d = x.shape
    ne = w_router.shape[-1]
    tg = _expert_row_tile(m_real, ne)
    comb, mask = _router(xb, w_router, b_router, layer)
    live = (jnp.arange(mp, dtype=jnp.int32) < m_real)[:, None]
    mask = jnp.where(live, mask, 0)
    col = jnp.arange(ne, dtype=jnp.int32)[None, :]
    rank = jnp.cumsum(mask, axis=0) - mask
    counts = jnp.sum(mask, axis=0)
    tiles_e = (counts + tg - 1) // tg
    tile_end = jnp.cumsum(tiles_e)
    tile_start = tile_end - tiles_e
    n_tiles = (TOP_K * m_real) // tg + ne
    p = n_tiles * tg
    pos = tile_start[None, :] * tg + rank
    e_hi = jnp.max(jnp.where(mask > 0, col, -1), axis=1, keepdims=True)
    e_lo = jnp.min(jnp.where(mask > 0, col, ne), axis=1, keepdims=True)
    pos_a = jnp.sum(jnp.where(col == e_hi, pos, 0), axis=1)
    pos_b = jnp.sum(jnp.where(col == e_lo, pos, 0), axis=1)
    gate_a = jnp.sum(jnp.where(col == e_hi, comb, 0.0), axis=1, keepdims=True)
    gate_b = jnp.sum(jnp.where(col == e_lo, comb, 0.0), axis=1, keepdims=True)
    spare = p + 2 * (jnp.arange(mp, dtype=jnp.int32) - m_real)
    send_a = jnp.where(live[:, 0], pos_a, spare)
    send_b = jnp.where(live[:, 0], pos_b, spare + 1)
    tile_id = jnp.arange(n_tiles, dtype=jnp.int32)
    n_active = tile_end[-1]
    tile_c = jnp.minimum(tile_id, jnp.maximum(n_active - 1, 0))
    te = jnp.minimum(jnp.sum((tile_end[None, :] <= tile_c[:, None]).astype(jnp.int32), axis=1), ne - 1)
    tv = jnp.clip(counts[te] - (tile_c - tile_start[te]) * tg, 0, tg)
    tv = jnp.where(tile_id < n_active, tv, 0).astype(jnp.int32)
    tb = tile_c.astype(jnp.int32)

    xs = _dispatch(send_a.astype(jnp.int32), send_b.astype(jnp.int32), x, p + 2 * (mp - m_real))
    y = _expert_mlp(te.astype(jnp.int32), tv, tb, xs, w_gate, w_up, w_down, layer, tg)
    return _combine_ln(pos_a.astype(jnp.int32), pos_b.astype(jnp.int32), x, gate_a, gate_b, ln_g, ln_b, y, alpha,
                       m_head)


def kernel(x_prompt, x_sample, cache_k, cache_v, cache_logf, state_conv, page_table, a_w_in, a_ln_g, a_ln_b, a_w_s, a_b_s, a_w_out, b_w_in, b_b_f, b_w_o, c_w_in, c_conv_w, c_w_out, ffn_w_gate, ffn_w_up, ffn_w_down, moe_w_router, moe_b_router, moe_w_gate, moe_w_up, moe_w_down, ln1_g, ln1_b, ln2_g, ln2_b):
    batch, seq, d = x_prompt.shape
    dec_batch, dec_seq, _ = x_sample.shape
    depth = ln1_g.shape[0]
    alpha = (2.0 * depth) ** 0.25
    m_prompt = batch * seq
    m_samp = dec_batch * dec_seq
    m_real = m_prompt + m_samp
    assert seq % TM == 0 and m_samp <= CHUNK and TM % CHUNK == 0 and dec_seq <= SUBLANES
    n_prompt_tiles = m_prompt // TM
    tiles_per_seq = seq // TM
    mp = m_prompt + TM
    n_pool = cache_k.shape[1]
    n_pages = page_table.shape[1]
    assert n_pages % PAGES_PER_STEP == 0 and n_pages % SUFFIX_PAGES_PER_STEP == 0
    dh = d // H_B
    e_a = a_w_out.shape[1]
    e_c = c_w_out.shape[1]
    g_a = e_a // H_A
    assert e_a == d and e_c == d and g_a % LANES == 0 and dh == LANES

    x = jnp.concatenate([x_prompt.reshape(m_prompt, d), x_sample.reshape(m_samp, d),
                         jnp.zeros((mp - m_real, d), F32)], axis=0)
    xb = x.astype(BF16)

    r = jnp.arange(TM)
    tri_prompt = r[:, None] >= r[None, :]
    tri_tail = tri_prompt & ((r[:, None] // dec_seq) == (r[None, :] // dec_seq))
    tri = jnp.stack([tri_prompt, tri_tail]).astype(BF16)
    rp = jnp.arange(PAGE_SIZE)
    tri_upper = (rp[None, :] > rp[:, None]).astype(BF16)

    a_w_out16 = a_w_out.astype(BF16)
    b_w_o16 = b_w_o.astype(BF16)
    c_w_out16 = c_w_out.astype(BF16)
    ffn_w_down16 = ffn_w_down.astype(BF16)

    k_p, v_p, lf_p, k_s, v_s, lf_s, conv_p, conv_s, chunk_v_s = [], [], [], [], [], [], [], [], []
    n_mix = 3
    for li in range(depth):
        kind = li % n_mix
        j = li // n_mix
        g1, b1 = ln1_g[li][None], ln1_b[li][None]
        g2, b2 = ln2_g[li][None], ln2_b[li][None]
        if kind == 0:
            nt = e_a // TN
            u, v = _proj(xb, [(a_w_in, j, 0), (a_w_in, j, nt)], _ep_gelu, [_tile_spec(), _tile_spec()],
                         [jax.ShapeDtypeStruct((mp, e_a), BF16), jax.ShapeDtypeStruct((mp, e_a), F32)],
                         n_col_tiles=nt, name="a_in")
            rc = jnp.arange(CHUNK)
            causal = rc[:, None] >= rc[None, :]
            w_prompt = jnp.where(causal[None], a_w_s[j], 0.0)
            same = (rc[:, None] // dec_seq) == (rc[None, :] // dec_seq)
            t_in = rc % dec_seq
            w_tail = jnp.where((causal & same)[None], a_w_s[j][:, t_in][:, :, t_in], 0.0)
            wmix = jnp.stack([w_prompt, w_tail]).astype(BF16)
            bias_p = jnp.repeat(a_b_s[j].T, g_a, axis=1)
            bias_t = jnp.repeat(a_b_s[j][:, t_in].T, g_a, axis=1)
            bias = jnp.stack([bias_p, bias_t])
            us, vs = _spatial(u, v, a_ln_g[j][None], a_ln_b[j][None], wmix, bias, n_prompt_tiles)
            chunk_v_s.append(vs[:m_samp].reshape(dec_batch, dec_seq, e_a))
            x, xb = _out_ln(us, a_w_out16, j, x, g1, b1, alpha, TM)
        elif kind == 1:
            nt = d // TN
            head_spec = pl.BlockSpec((TM, TN), lambda jj, i: (jnp.minimum(i, n_prompt_tiles - 1), jj))
            tail_spec = pl.BlockSpec((TM, TN), lambda jj, i: (jnp.maximum(i - n_prompt_tiles, 0), jj))
            full16 = jax.ShapeDtypeStruct((mp, d), BF16)
            head32 = jax.ShapeDtypeStruct((m_prompt, d), F32)
            tail32 = jax.ShapeDtypeStruct((mp - m_prompt, d), F32)
            q16, k16, v16, k32p, k32t, v32p, v32t = _proj(
                xb, [(b_w_in, j, 0), (b_w_in, j, nt), (b_w_in, j, 2 * nt)],
                functools.partial(_ep_qkv, n_prompt_tiles=n_prompt_tiles),
                [_tile_spec(), _tile_spec(), _tile_spec(), head_spec, tail_spec, head_spec, tail_spec],
                [full16, full16, full16, head32, tail32, head32, tail32], n_col_tiles=nt, name="b_in_qkv")
            w_f = b_w_in[j][:, 3 * d:]
            lf, c, c_lanes = _logf(xb, w_f, b_b_f[j][None, :], tri, n_prompt_tiles, tiles_per_seq)
            scale = dh ** -0.5
            c_t = c.T
            o_prompt = _fox_prompt(q16, k16, v16[:m_prompt].T, c_t[:, None, :m_prompt], c_lanes, batch, seq, scale)

            cache_k_rows = cache_k.reshape(-1, dh)
            cache_v_rows = cache_v.reshape(-1, dh)
            cache_lf_rows = cache_logf.reshape(-1, H_B)
            ck_past = _suffix(page_table, cache_lf_rows, j, n_pool, tri_upper)
            qpad = ((0, 0), (0, 0), (0, Q_ROWS - dec_seq), (0, 0))
            q_s = q16[m_prompt:m_real].reshape(dec_batch, dec_seq, H_B, dh).transpose(0, 2, 1, 3)
            q_s = jnp.pad(q_s, qpad).reshape(dec_batch, H_B * Q_ROWS, dh)
            c_s = c[m_prompt:m_real].reshape(dec_batch, dec_seq, H_B, 1).transpose(0, 2, 1, 3)
            cq_s = jnp.pad(c_s, qpad).reshape(dec_batch, H_B * Q_ROWS, 1)
            o_samp = _fox_decode(page_table, q_s, cq_s, k32t, v32t, c_t[:, m_prompt:m_prompt + CHUNK], ck_past,
                                 cache_k_rows, cache_v_rows, j, n_pool, scale, dec_seq)
            o_samp = o_samp.reshape(dec_batch, H_B, Q_ROWS, dh)[:, :, :dec_seq].transpose(0, 2, 1, 3)
            o_samp = o_samp.reshape(m_samp, d).astype(BF16)
            o_all = jnp.concatenate([o_prompt, o_samp, jnp.zeros((mp - m_real, d), BF16)], axis=0)
            x, xb = _out_ln(o_all, b_w_o16, j, x, g1, b1, alpha, TM)
            k_p.append(k32p.reshape(batch, seq, H_B, dh))
            v_p.append(v32p.reshape(batch, seq, H_B, dh))
            lf_p.append(lf[:m_prompt].reshape(batch, seq, H_B))
            k_s.append(k32t[:m_samp].reshape(dec_batch, dec_seq, H_B, dh))
            v_s.append(v32t[:m_samp].reshape(dec_batch, dec_seq, H_B, dh))
            lf_s.append(lf[m_prompt:m_real].reshape(dec_batch, dec_seq, H_B))
        else:
            nt = e_c // TN
            st = state_conv[j]
            p1 = jnp.zeros((dec_batch, dec_seq, e_c), F32).at[:, 0].set(st[:, 1])
            p2 = jnp.zeros((dec_batch, dec_seq, e_c), F32).at[:, 0].set(st[:, 0]).at[:, 1].set(st[:, 1])
            pad = jnp.zeros((TM - m_samp, e_c), F32)
            p1 = jnp.concatenate([p1.reshape(m_samp, e_c), pad], axis=0)
            p2 = jnp.concatenate([p2.reshape(m_samp, e_c), pad], axis=0)
            n_tiles = mp // TM
            ep = functools.partial(_ep_conv, tiles_per_seq=tiles_per_seq, n_prompt_tiles=n_prompt_tiles,
                                   dec_seq=dec_seq)
            bgy, ztail, zs = _proj(
                xb, [(c_w_in, j, 0), (c_w_in, j, nt), (c_w_in, j, 2 * nt)], ep,
                [_tile_spec(), pl.BlockSpec((SUBLANES, TN), lambda jj, i: (i, jj)),
                 pl.BlockSpec((TM, TN), lambda jj, i: (0, jj))],
                [jax.ShapeDtypeStruct((mp, e_c), BF16), jax.ShapeDtypeStruct((n_tiles * SUBLANES, e_c), F32),
                 jax.ShapeDtypeStruct((TM, e_c), F32)],
                n_col_tiles=nt, name="c_in_conv",
                extra_in=[c_conv_w, p1, p2],
                extra_specs=[pl.BlockSpec((None, CONV_W, TN), lambda jj, i: (j, 0, jj)),
                             pl.BlockSpec((TM, TN), lambda jj, i: (0, jj)),
                             pl.BlockSpec((TM, TN), lambda jj, i: (0, jj))],
                extra_scratch=[pltpu.VMEM((SUBLANES, TN), F32)])
            zt = ztail.reshape(n_tiles, SUBLANES, e_c)
            last_tiles = jnp.arange(batch) * tiles_per_seq + tiles_per_seq - 1
            conv_p.append(zt[last_tiles][:, SUBLANES - (CONV_W - 1):, :])
            conv_s.append(zs[:m_samp].reshape(dec_batch, dec_seq, e_c)[:, dec_seq - (CONV_W - 1):, :])
            x, xb = _out_ln(bgy, c_w_out16, j, x, g1, b1, alpha, TM)

        f = li // 2
        if li % 2 == 0:
            nt = ffn_w_gate.shape[-1] // TN
            (h,) = _proj(xb, [(ffn_w_gate, f, 0), (ffn_w_up, f, 0)], _ep_swiglu, [_tile_spec()],
                         [jax.ShapeDtypeStruct((mp, ffn_w_gate.shape[-1]), BF16)], n_col_tiles=nt, name="ffn_up")
            x, xb = _out_ln(h, ffn_w_down16, f, x, g2, b2, alpha, TM_WIDE_K)
        else:
            last = li == depth - 1
            x, xb = _moe(x, xb, m_real, moe_w_router, moe_b_router[f][None], moe_w_gate, moe_w_up, moe_w_down, f,
                         g2, b2, alpha, m_prompt if last else None)

    if isinstance(x, tuple):
        x_head, x_tail = x
    else:
        x_head, x_tail = x[:m_prompt], x[m_prompt:]
    y_prompt = x_head.reshape(batch, seq, d)
    y_sample = x_tail[:m_samp].reshape(dec_batch, dec_seq, d)
    return (y_prompt, y_sample, jnp.stack(k_p), jnp.stack(v_p), jnp.stack(lf_p), jnp.stack(k_s),
            jnp.stack(v_s), jnp.stack(lf_s), jnp.stack(conv_p), jnp.stack(conv_s), jnp.stack(chunk_v_s))
```

```python
import functools

import jax
import jax.numpy as jnp
from jax import lax
from jax.experimental import pallas as pl
from jax.experimental.pallas import tpu as pltpu

F32 = jnp.float32
BF16 = jnp.bfloat16
U32 = jnp.uint32

H_A = 16
CHUNK = 128
H_B = 16
PAGE_SIZE = 128
CONV_W = 3
TOP_K = 2
LN_EPS = 1e-5
NEG_INF = -1e30

SUBLANES = 8
LANES = 128

TM = 512
TAIL_ROWS = 128
TN = 512
TM_WIDE_K = 256
TQ = 512
TKV = 512
HEADS_PER_STEP = 2
PAGES_PER_STEP = 4
SUFFIX_PAGES_PER_STEP = 8
Q_ROWS = 16
TF = 256
N_SUB = 8
TC = 256
DMA_UNROLL = 8

VMEM_LIMIT = 56 * 1024 * 1024


def _cparams(*sem):
    return pltpu.CompilerParams(dimension_semantics=sem, vmem_limit_bytes=VMEM_LIMIT)


def _layer_norm_rows(y, g, b):
    mu = jnp.mean(y, axis=-1, keepdims=True)
    d = y - mu
    var = jnp.mean(d * d, axis=-1, keepdims=True)
    return d * lax.rsqrt(var + LN_EPS) * g + b


def _proj_kernel(*refs, n_w, n_extra, n_out, epilogue, n_full_tiles):
    x_ref = refs[0]
    w_refs = refs[1:1 + n_w]
    extra = refs[1 + n_w:1 + n_w + n_extra]
    outs = refs[1 + n_w + n_extra:1 + n_w + n_extra + n_out]
    scratch = refs[1 + n_w + n_extra + n_out:]
    wb_refs = scratch[:n_w]
    rest = scratch[n_w:]
    i = pl.program_id(1)

    @pl.when(i == 0)
    def _():
        for w_ref, wb_ref in zip(w_refs, wb_refs):
            wb_ref[...] = w_ref[...].astype(BF16)

    def run(rows):
        x = x_ref[0:rows, :]
        accs = [jnp.dot(x, wb_ref[...], preferred_element_type=F32) for wb_ref in wb_refs]
        epilogue(accs, extra, outs, rest, rows)

    @pl.when(i < n_full_tiles)
    def _():
        run(TM)

    @pl.when(i >= n_full_tiles)
    def _():
        run(TAIL_ROWS)


def _store_rows(o_ref, val, rows):
    o_ref[0:rows, :] = val.astype(o_ref.dtype)
    if rows < o_ref.shape[0]:
        o_ref[rows:, :] = jnp.zeros((o_ref.shape[0] - rows, o_ref.shape[1]), o_ref.dtype)


def _proj(x, weights, epilogue, out_specs, out_shapes, *, n_col_tiles, n_full_tiles, name, extra_in=(),
          extra_specs=(), extra_scratch=(), tn=TN):
    mp, k = x.shape
    n_w = len(weights)
    in_specs = [pl.BlockSpec((TM, k), lambda j, i: (i, 0))]
    args = [x]
    for w, layer, off in weights:
        in_specs.append(pl.BlockSpec((None, k, tn), functools.partial(
            lambda j, i, layer, off: (layer, 0, j + off), layer=layer, off=off)))
        args.append(w)
    in_specs += list(extra_specs)
    args += list(extra_in)
    kern = functools.partial(_proj_kernel, n_w=n_w, n_extra=len(extra_in), n_out=len(out_shapes),
                             epilogue=epilogue, n_full_tiles=n_full_tiles)
    return pl.pallas_call(
        kern,
        grid=(n_col_tiles, mp // TM),
        in_specs=in_specs,
        out_specs=out_specs,
        out_shape=out_shapes,
        scratch_shapes=[pltpu.VMEM((k, tn), BF16) for _ in range(n_w)] + list(extra_scratch),
        compiler_params=_cparams("arbitrary", "arbitrary"),
        name=name,
    )(*args)


def _tile_spec(tn=TN):
    return pl.BlockSpec((TM, tn), lambda j, i: (i, j))


def _ep_gelu(accs, extra, outs, rest, rows):
    for acc, o in zip(accs, outs):
        _store_rows(o, jax.nn.gelu(acc), rows)


def _ep_qkv(accs, extra, outs, rest, rows):
    q16, k16, v16, k32p, k32t, v32p, v32t = outs
    _store_rows(q16, accs[0], rows)
    _store_rows(k16, accs[1], rows)
    _store_rows(v16, accs[2], rows)
    if rows == TM:
        k32p[...] = accs[1]
        v32p[...] = accs[2]
    else:
        _store_rows(k32t, accs[1], rows)
        _store_rows(v32t, accs[2], rows)


def _ep_swiglu(accs, extra, outs, rest, rows):
    _store_rows(outs[0], jax.nn.silu(accs[0]) * accs[1], rows)


def _ep_conv(accs, extra, outs, rest, rows, *, tiles_per_seq, dec_seq):
    cw_ref, p1_ref, p2_ref = extra
    y_ref, ztail_ref, zs_ref = outs
    (carry_ref,) = rest
    i = pl.program_id(1)
    z = accs[1] * accs[2]
    row = lax.broadcasted_iota(jnp.int32, z.shape, 0)
    r1 = pltpu.roll(z, 1, 0)
    r2 = pltpu.roll(z, 2, 0)
    w0 = cw_ref[0:1, :]
    w1 = cw_ref[1:2, :]
    w2 = cw_ref[2:3, :]
    if rows == TM:
        @pl.when(i % tiles_per_seq == 0)
        def _():
            carry_ref[...] = jnp.zeros_like(carry_ref)

        c6 = carry_ref[6:7, :]
        c7 = carry_ref[7:8, :]
        s1 = jnp.where(row == 0, c7, r1)
        s2 = jnp.where(row == 0, c6, jnp.where(row == 1, c7, r2))
        y_ref[...] = (accs[0] * (w0 * s2 + w1 * s1 + w2 * z)).astype(y_ref.dtype)
        carry_ref[...] = z[TM - SUBLANES:, :]
    else:
        t = row % dec_seq
        s1 = jnp.where(t >= 1, r1, p1_ref[0:rows, :])
        s2 = jnp.where(t >= 2, r2, p2_ref[0:rows, :])
        _store_rows(y_ref, accs[0] * (w0 * s2 + w1 * s1 + w2 * z), rows)
        _store_rows(zs_ref, z, rows)
    ztail_ref[...] = z[rows - SUBLANES:, :]


def _out_ln_kernel(x_ref, w_ref, r_ref, g_ref, b_ref, o_ref, ob_ref, *, alpha, n_full_tiles):
    i = pl.program_id(0)

    def run(rows):
        acc = jnp.dot(x_ref[0:rows, :], w_ref[...], preferred_element_type=F32)
        y = _layer_norm_rows(alpha * r_ref[0:rows, :] + acc, g_ref[...], b_ref[...])
        _store_rows(o_ref, y, rows)
        _store_rows(ob_ref, y, rows)

    @pl.when(i < n_full_tiles)
    def _():
        run(o_ref.shape[0])

    @pl.when(i == n_full_tiles)
    def _():
        run(TAIL_ROWS)

    @pl.when(i > n_full_tiles)
    def _():
        o_ref[...] = jnp.zeros_like(o_ref)
        ob_ref[...] = jnp.zeros_like(ob_ref)


def _out_ln(x, w, layer, resid, g, b, alpha, tm, m_full):
    mp, k = x.shape
    d = resid.shape[1]
    row = lambda i: (i, 0)
    fixed = lambda i: (0, 0)
    return pl.pallas_call(
        functools.partial(_out_ln_kernel, alpha=alpha, n_full_tiles=m_full // tm),
        grid=(mp // tm,),
        in_specs=[
            pl.BlockSpec((tm, k), row),
            pl.BlockSpec((None, k, d), lambda i: (layer, 0, 0), pipeline_mode=pl.Buffered(1)),
            pl.BlockSpec((tm, d), row),
            pl.BlockSpec((1, d), fixed),
            pl.BlockSpec((1, d), fixed),
        ],
        out_specs=[pl.BlockSpec((tm, d), row), pl.BlockSpec((tm, d), row)],
        out_shape=[jax.ShapeDtypeStruct((mp, d), F32), jax.ShapeDtypeStruct((mp, d), BF16)],
        compiler_params=_cparams("arbitrary"),
        name="out_ln",
    )(x, w, resid, g, b)


def _spatial_kernel(u_ref, v_ref, g_ref, b_ref, wmix_ref, bias_ref, o_ref, vs_ref, vln_ref, *, n_prompt_tiles):
    i = pl.program_id(0)
    vln = _layer_norm_rows(v_ref[...], g_ref[...], b_ref[...])
    vln_ref[...] = vln.astype(BF16)

    @pl.when(i >= n_prompt_tiles)
    def _():
        vs_ref[...] = vln

    g_a = u_ref.shape[1] // H_A

    def chunk_body(c, carry):
        r0 = pl.multiple_of(c * CHUNK, CHUNK)
        for h in range(H_A):
            cols = slice(h * g_a, (h + 1) * g_a)
            s = jnp.dot(wmix_ref[h], vln_ref[pl.ds(r0, CHUNK), cols], preferred_element_type=F32)
            s = s + bias_ref[:, cols]
            o_ref[pl.ds(r0, CHUNK), cols] = (u_ref[pl.ds(r0, CHUNK), cols].astype(F32) * s).astype(o_ref.dtype)
        return carry

    lax.fori_loop(0, TM // CHUNK, chunk_body, 0)


def _spatial(u, v, ln_g, ln_b, wmix, bias, n_prompt_tiles):
    mp, e = u.shape
    sel = lambda i: jnp.where(i < n_prompt_tiles, 0, 1)
    return pl.pallas_call(
        functools.partial(_spatial_kernel, n_prompt_tiles=n_prompt_tiles),
        grid=(mp // TM,),
        in_specs=[
            pl.BlockSpec((TM, e), lambda i: (i, 0)),
            pl.BlockSpec((TM, e), lambda i: (i, 0)),
            pl.BlockSpec((1, e), lambda i: (0, 0)),
            pl.BlockSpec((1, e), lambda i: (0, 0)),
            pl.BlockSpec((None, H_A, CHUNK, CHUNK), lambda i: (sel(i), 0, 0, 0)),
            pl.BlockSpec((None, CHUNK, e), lambda i: (sel(i), 0, 0)),
        ],
        out_specs=[pl.BlockSpec((TM, e), lambda i: (i, 0)), pl.BlockSpec((TM, e), lambda i: (0, 0))],
        out_shape=[jax.ShapeDtypeStruct((mp, e), BF16), jax.ShapeDtypeStruct((TM, e), F32)],
        scratch_shapes=[pltpu.VMEM((TM, e), BF16)],
        compiler_params=_cparams("arbitrary"),
        name="spatial_mix",
    )(u, v, ln_g, ln_b, wmix, bias)


def _split3(x):
    hi = x.astype(BF16)
    r1 = x - hi.astype(F32)
    mid = r1.astype(BF16)
    lo = (r1 - mid.astype(F32)).astype(BF16)
    return hi, mid, lo


def _tri_matmul(tri, x):
    hi, mid, lo = _split3(x)
    out = jnp.dot(tri, lo, preferred_element_type=F32)
    out = out + jnp.dot(tri, mid, preferred_element_type=F32)
    return out + jnp.dot(tri, hi, preferred_element_type=F32)


def _logf_kernel(x_ref, w_ref, bf_ref, tri_ref, lf_ref, c_ref, cb_ref, carry_ref, *, tiles_per_seq):
    i = pl.program_id(0)

    @pl.when(i % tiles_per_seq == 0)
    def _():
        carry_ref[...] = jnp.zeros_like(carry_ref)

    z = jnp.dot(x_ref[...], w_ref[...].astype(BF16), preferred_element_type=F32) + bf_ref[...]
    lf = jax.nn.log_sigmoid(z)
    lf_ref[...] = lf
    c = _tri_matmul(tri_ref[...], lf) + carry_ref[0:1, :]
    c_ref[...] = c
    carry_ref[0:1, :] = c[TM - 1:TM, :]
    for h in range(c.shape[1]):
        cb_ref[h] = jnp.broadcast_to(c[:, h:h + 1], (TM, LANES))


def _logf(xb, w_f, b_f, tri, n_prompt_tiles, tiles_per_seq):
    mp, d = xb.shape
    h = w_f.shape[1]
    sel = lambda i: jnp.where(i < n_prompt_tiles, 0, 1)
    return pl.pallas_call(
        functools.partial(_logf_kernel, tiles_per_seq=tiles_per_seq),
        grid=(mp // TM,),
        in_specs=[
            pl.BlockSpec((TM, d), lambda i: (i, 0)),
            pl.BlockSpec((d, h), lambda i: (0, 0)),
            pl.BlockSpec((1, h), lambda i: (0, 0)),
            pl.BlockSpec((None, TM, TM), lambda i: (sel(i), 0, 0)),
        ],
        out_specs=[pl.BlockSpec((TM, h), lambda i: (i, 0)), pl.BlockSpec((TM, h), lambda i: (i, 0)),
                   pl.BlockSpec((h, TM, LANES), lambda i: (0, i, 0))],
        out_shape=[jax.ShapeDtypeStruct((mp, h), F32), jax.ShapeDtypeStruct((mp, h), F32),
                   jax.ShapeDtypeStruct((h, mp, LANES), F32)],
        scratch_shapes=[pltpu.VMEM((SUBLANES, h), F32)],
        compiler_params=_cparams("arbitrary"),
        name="logf_cumsum",
    )(xb, w_f, b_f, tri)


def _fox_prompt_kernel(q_ref, k_ref, vt_ref, cq_ref, ckb_ref, o_ref, *, scale):
    qi = pl.program_id(2)
    tq = q_ref.shape[0]
    dh = q_ref.shape[1] // HEADS_PER_STEP
    heads = range(HEADS_PER_STEP)
    qs = [q_ref[:, g * dh:(g + 1) * dh] for g in heads]
    cqs = [cq_ref[g] for g in heads]

    def step(g, kv0, carry, masked):
        m, l, acc = carry
        k = k_ref[pl.ds(kv0, TKV), g * dh:(g + 1) * dh]
        s = lax.dot_general(k, qs[g], (((1,), (1,)), ((), ())), preferred_element_type=F32)
        ckb = ckb_ref[g, pl.ds(kv0, TKV), :]
        s = s * scale + cqs[g] - jnp.concatenate([ckb] * (tq // LANES), axis=1)
        if masked:
            key = lax.broadcasted_iota(jnp.int32, s.shape, 0)
            qrow = lax.broadcasted_iota(jnp.int32, s.shape, 1)
            s = jnp.where(qrow >= key, s, NEG_INF)
        m_new = jnp.maximum(m, jnp.max(s, axis=0, keepdims=True))
        a = jnp.exp(m - m_new)
        p = jnp.exp(s - m_new)
        l = a * l + jnp.sum(p, axis=0, keepdims=True)
        vt = vt_ref[g * dh:(g + 1) * dh, pl.ds(kv0, TKV)]
        acc = a * acc + jnp.dot(vt, p.astype(BF16), preferred_element_type=F32)
        return m_new, l, acc

    init = tuple((jnp.full((1, tq), NEG_INF, F32), jnp.zeros((1, tq), F32), jnp.zeros((dh, tq), F32))
                 for _ in heads)

    def body(j, carries):
        kv0 = pl.multiple_of(j * TKV, TKV)
        return tuple(step(g, kv0, carries[g], False) for g in heads)

    carries = lax.fori_loop(0, qi * (tq // TKV), body, init)
    for g in heads:
        m, l, acc = step(g, pl.multiple_of(qi * tq, tq), carries[g], True)
        o_ref[:, g * dh:(g + 1) * dh] = (acc / l).T.astype(o_ref.dtype)


def _fox_prompt(q, k, vt, cq, ckb, batch, seq, scale):
    dh = q.shape[1] // H_B
    nq = seq // TQ
    w = HEADS_PER_STEP * dh
    return pl.pallas_call(
        functools.partial(_fox_prompt_kernel, scale=scale),
        grid=(batch, H_B // HEADS_PER_STEP, nq),
        in_specs=[
            pl.BlockSpec((TQ, w), lambda b, h, i: (b * nq + i, h)),
            pl.BlockSpec((seq, w), lambda b, h, i: (b, h)),
            pl.BlockSpec((w, seq), lambda b, h, i: (h, b)),
            pl.BlockSpec((HEADS_PER_STEP, 1, TQ), lambda b, h, i: (h, 0, b * nq + i)),
            pl.BlockSpec((HEADS_PER_STEP, seq, LANES), lambda b, h, i: (h, b, 0)),
        ],
        out_specs=pl.BlockSpec((TQ, w), lambda b, h, i: (b * nq + i, h)),
        out_shape=jax.ShapeDtypeStruct((batch * seq, H_B * dh), BF16),
        compiler_params=_cparams("arbitrary", "arbitrary", "arbitrary"),
        name="fox_prompt",
    )(q, k, vt, cq, ckb)


def _suffix_kernel(pt_ref, *refs):
    sp = SUFFIX_PAGES_PER_STEP
    lf_refs = refs[:sp]
    tri_ref, o_ref, carry_ref = refs[sp:]

    @pl.when(pl.program_id(1) == 0)
    def _():
        carry_ref[...] = jnp.zeros_like(carry_ref)

    carry = carry_ref[...]
    tri = tri_ref[...]
    ones = jnp.ones((PAGE_SIZE, LANES), BF16)
    for r in reversed(range(sp)):
        within = jnp.zeros(carry.shape, F32)
        total = jnp.zeros(carry.shape, F32)
        for part in reversed(_split3(lf_refs[r][...])):
            within = within + lax.dot_general(part, tri, (((0,), (1,)), ((), ())), preferred_element_type=F32)
            total = total + lax.dot_general(part, ones, (((0,), (0,)), ((), ())), preferred_element_type=F32)
        o_ref[:, r * PAGE_SIZE:(r + 1) * PAGE_SIZE] = -(within + carry)
        carry = carry + total
    carry_ref[...] = carry


def _suffix(page_table, cache_lf_rows, layer, n_pool, tri_upper):
    n, n_pages = page_table.shape
    h = cache_lf_rows.shape[-1]
    sp = SUFFIX_PAGES_PER_STEP
    groups = n_pages // sp

    def page_spec(r):
        return pl.BlockSpec((PAGE_SIZE, h),
                            lambda b, p, pt: (layer * n_pool + pt[b, (groups - 1 - p) * sp + r], 0))

    grid_spec = pltpu.PrefetchScalarGridSpec(
        num_scalar_prefetch=1,
        grid=(n, groups),
        in_specs=[page_spec(r) for r in range(sp)] + [
            pl.BlockSpec((PAGE_SIZE, PAGE_SIZE), lambda b, p, pt: (0, 0))],
        out_specs=pl.BlockSpec((None, h, sp * PAGE_SIZE), lambda b, p, pt: (b, 0, groups - 1 - p)),
        scratch_shapes=[pltpu.VMEM((h, LANES), F32)],
    )
    return pl.pallas_call(
        _suffix_kernel,
        grid_spec=grid_spec,
        out_shape=jax.ShapeDtypeStruct((n, h, n_pages * PAGE_SIZE), F32),
        compiler_params=_cparams("arbitrary", "arbitrary"),
        name="logf_suffix",
    )(page_table, *([cache_lf_rows] * sp), tri_upper)


def _fox_decode_kernel(pt_ref, q_ref, cq_ref, knew_ref, vnew_ref, cknew_ref, ckpast_ref, *rest, scale, dec_seq):
    pp = PAGES_PER_STEP
    kp_refs = rest[:pp]
    vp_refs = rest[pp:2 * pp]
    o_ref = rest[2 * pp]
    s_ref, p_ref, m_ref, l_ref, acc_ref = rest[2 * pp + 1:]
    n = pl.program_id(0)
    pg = pl.program_id(1)
    dh = q_ref.shape[1]
    nt_dims = (((1,), (1,)), ((), ()))

    def hrows(h):
        return slice(h * Q_ROWS, (h + 1) * Q_ROWS)

    def softmax_update(width):
        s = s_ref[:, :width]
        m_old = m_ref[...]
        m_new = jnp.maximum(m_old, jnp.max(s, axis=-1, keepdims=True))
        a = jnp.exp(m_old - m_new)
        p = jnp.exp(s - m_new)
        l_ref[...] = a * l_ref[...] + jnp.sum(p, axis=-1, keepdims=True)
        m_ref[...] = m_new
        p_ref[:, :width] = p.astype(BF16)
        acc_ref[...] = a * acc_ref[...]

    @pl.when(pg == 0)
    def _():
        m_ref[...] = jnp.full(m_ref.shape, NEG_INF, F32)
        l_ref[...] = jnp.zeros_like(l_ref)
        acc_ref[...] = jnp.zeros_like(acc_ref)
        nk = knew_ref.shape[0]
        t = lax.broadcasted_iota(jnp.int32, (Q_ROWS, nk), 0)
        key = lax.broadcasted_iota(jnp.int32, (Q_ROWS, nk), 1)
        mask = (key >= n * dec_seq) & (key <= n * dec_seq + t)
        for h in range(H_B):
            k_h = knew_ref[:, h * dh:(h + 1) * dh].astype(BF16)
            s = lax.dot_general(q_ref[hrows(h), :], k_h, nt_dims, preferred_element_type=F32)
            s = s * scale + cq_ref[hrows(h), :] - cknew_ref[h:h + 1, :]
            s_ref[hrows(h), 0:nk] = jnp.where(mask, s, NEG_INF)
        softmax_update(nk)
        for h in range(H_B):
            v_h = vnew_ref[:, h * dh:(h + 1) * dh].astype(BF16)
            acc_ref[hrows(h), :] += jnp.dot(p_ref[hrows(h), 0:nk], v_h, preferred_element_type=F32)

    for r in range(pp):
        cols = slice(r * PAGE_SIZE, (r + 1) * PAGE_SIZE)
        for h in range(H_B):
            k_h = kp_refs[r][pl.ds(h, PAGE_SIZE, stride=H_B), :].astype(BF16)
            s = lax.dot_general(q_ref[hrows(h), :], k_h, nt_dims, preferred_element_type=F32)
            s_ref[hrows(h), cols] = s * scale + cq_ref[hrows(h), :] - ckpast_ref[h:h + 1, cols]
    softmax_update(pp * PAGE_SIZE)
    for h in range(H_B):
        pv = None
        for r in range(pp):
            cols = slice(r * PAGE_SIZE, (r + 1) * PAGE_SIZE)
            v_h = vp_refs[r][pl.ds(h, PAGE_SIZE, stride=H_B), :].astype(BF16)
            d = jnp.dot(p_ref[hrows(h), cols], v_h, preferred_element_type=F32)
            pv = d if pv is None else pv + d
        acc_ref[hrows(h), :] += pv

    @pl.when(pg == pl.num_programs(1) - 1)
    def _():
        o_ref[...] = acc_ref[...] / l_ref[...]


def _fox_decode(page_table, q_s, cq_s, k_new, v_new, ck_new, ck_past, cache_k_rows, cache_v_rows, layer, n_pool,
                scale, dec_seq):
    n, n_pages = page_table.shape
    rows, dh = q_s.shape[1:]
    d = k_new.shape[1]
    nk = ck_new.shape[1]
    pp = PAGES_PER_STEP
    steps = n_pages // pp
    page_rows = PAGE_SIZE * H_B

    def page_spec(r):
        return pl.BlockSpec((page_rows, dh), lambda b, p, pt: (layer * n_pool + pt[b, p * pp + r], 0))

    in_specs = [
        pl.BlockSpec((None, rows, dh), lambda b, p, pt: (b, 0, 0)),
        pl.BlockSpec((None, rows, 1), lambda b, p, pt: (b, 0, 0)),
        pl.BlockSpec((nk, d), lambda b, p, pt: (0, 0)),
        pl.BlockSpec((nk, d), lambda b, p, pt: (0, 0)),
        pl.BlockSpec((H_B, nk), lambda b, p, pt: (0, 0)),
        pl.BlockSpec((None, H_B, pp * PAGE_SIZE), lambda b, p, pt: (b, 0, p)),
    ]
    in_specs += [page_spec(r) for r in range(pp)]
    in_specs += [page_spec(r) for r in range(pp)]
    grid_spec = pltpu.PrefetchScalarGridSpec(
        num_scalar_prefetch=1,
        grid=(n, steps),
        in_specs=in_specs,
        out_specs=pl.BlockSpec((None, rows, dh), lambda b, p, pt: (b, 0, 0)),
        scratch_shapes=[pltpu.VMEM((rows, pp * PAGE_SIZE), F32), pltpu.VMEM((rows, pp * PAGE_SIZE), BF16),
                        pltpu.VMEM((rows, 1), F32), pltpu.VMEM((rows, 1), F32), pltpu.VMEM((rows, dh), F32)],
    )
    return pl.pallas_call(
        functools.partial(_fox_decode_kernel, scale=scale, dec_seq=dec_seq),
        grid_spec=grid_spec,
        out_shape=jax.ShapeDtypeStruct((n, rows, dh), F32),
        compiler_params=_cparams("arbitrary", "arbitrary"),
        name="fox_decode",
    )(page_table, q_s, cq_s, k_new, v_new, ck_new, ck_past,
      *([cache_k_rows] * pp), *([cache_v_rows] * pp))


def _router_kernel(x_ref, w_ref, b_ref, comb_ref, mask_ref):
    logits = jnp.dot(x_ref[...], w_ref[...].astype(BF16), preferred_element_type=F32) + b_ref[...]
    ne = logits.shape[1]
    col = lax.broadcasted_iota(jnp.int32, logits.shape, 1)
    m1 = jnp.max(logits, axis=-1, keepdims=True)
    i1 = jnp.min(jnp.where(logits == m1, col, ne), axis=-1, keepdims=True)
    rest = jnp.where(col == i1, -jnp.inf, logits)
    m2 = jnp.max(rest, axis=-1, keepdims=True)
    i2 = jnp.min(jnp.where(rest == m2, col, ne), axis=-1, keepdims=True)
    e2 = jnp.exp(m2 - m1)
    den = 1.0 + e2
    comb_ref[...] = jnp.where(col == i1, 1.0 / den, 0.0) + jnp.where(col == i2, e2 / den, 0.0)
    mask_ref[...] = ((col == i1) | (col == i2)).astype(jnp.int32)


def _router(xb, w_r, b_r, layer):
    mp, d = xb.shape
    ne = w_r.shape[-1]
    return pl.pallas_call(
        _router_kernel,
        grid=(mp // TM,),
        in_specs=[
            pl.BlockSpec((TM, d), lambda i: (i, 0)),
            pl.BlockSpec((None, d, ne), lambda i: (layer, 0, 0)),
            pl.BlockSpec((1, ne), lambda i: (0, 0)),
        ],
        out_specs=[pl.BlockSpec((TM, ne), lambda i: (i, 0)), pl.BlockSpec((TM, ne), lambda i: (i, 0))],
        out_shape=[jax.ShapeDtypeStruct((mp, ne), F32), jax.ShapeDtypeStruct((mp, ne), jnp.int32)],
        compiler_params=_cparams("arbitrary"),
        name="router",
    )(xb, w_r, b_r)


def _row_copy(src, dst, src_row, dst_row, sem):
    return pltpu.make_async_copy(src.at[pl.ds(src_row, 1)], dst.at[pl.ds(dst_row, 1)], sem)


def _dispatch_kernel(pa_ref, pb_ref, x_ref, xs_in_ref, xs_ref, buf_ref, sem_ref):
    del xs_in_ref
    i = pl.program_id(0)
    slot = i % 2
    half = x_ref.shape[1] // 2
    buf = buf_ref.at[slot]
    sem = sem_ref.at[slot]

    def drain(b, s):
        for _ in range(2):
            pltpu.make_async_copy(b, xs_ref.at[pl.ds(0, TM)], s).wait()

    @pl.when(i >= 2)
    def _():
        drain(buf, sem)

    x = x_ref[...]
    hi = x[:, :half].astype(BF16).astype(F32)
    lo = x[:, half:].astype(BF16).astype(F32)
    buf[...] = pltpu.bitcast(hi, U32) | (pltpu.bitcast(lo, U32) >> 16)

    def send(r, c):
        t = i * TM + r
        _row_copy(buf, xs_ref, r, pa_ref[t], sem).start()
        _row_copy(buf, xs_ref, r, pb_ref[t], sem).start()
        return c

    lax.fori_loop(0, TM, send, 0, unroll=DMA_UNROLL)

    @pl.when(i == pl.num_programs(0) - 1)
    def _():
        drain(buf, sem)

        @pl.when(i >= 1)
        def _():
            drain(buf_ref.at[1 - slot], sem_ref.at[1 - slot])


def _dispatch(pos_a, pos_b, x, n_rows):
    mp, d = x.shape
    words = d // 2
    assert words % LANES == 0
    xs0 = jnp.zeros((n_rows, words), U32)
    grid_spec = pltpu.PrefetchScalarGridSpec(
        num_scalar_prefetch=2,
        grid=(mp // TM,),
        in_specs=[pl.BlockSpec((TM, d), lambda i, pa, pb: (i, 0)), pl.BlockSpec(memory_space=pl.ANY)],
        out_specs=pl.BlockSpec(memory_space=pl.ANY),
        scratch_shapes=[pltpu.VMEM((2, TM, words), U32), pltpu.SemaphoreType.DMA((2,))],
    )
    return pl.pallas_call(
        _dispatch_kernel,
        grid_spec=grid_spec,
        out_shape=jax.ShapeDtypeStruct(xs0.shape, U32),
        input_output_aliases={3: 0},
        compiler_params=_cparams("arbitrary"),
        name="moe_dispatch",
    )(pos_a, pos_b, x, xs0)


def _expert_mlp_kernel(te_ref, tv_ref, tb_ref, xs_ref, wg_ref, wu_ref, wd_ref, o_ref, xb_ref):
    i = pl.program_id(0)
    f = pl.program_id(1)
    valid = tv_ref[i]
    tg = o_ref.shape[0]
    sb = tg // N_SUB
    half = o_ref.shape[1] // 2

    @pl.when((f == 0) & (valid > 0))
    def _():
        u = xs_ref[...]
        xb_ref[:, :half] = pltpu.bitcast(u & jnp.uint32(0xFFFF0000), F32).astype(BF16)
        xb_ref[:, half:] = pltpu.bitcast(u << 16, F32).astype(BF16)
        o_ref[...] = jnp.zeros_like(o_ref)

    @pl.when((f == 0) & (valid == 0))
    def _():
        o_ref[...] = jnp.zeros_like(o_ref)

    def mlp(rows):
        x = xb_ref[rows, :]
        g = jnp.dot(x, wg_ref[...].astype(BF16), preferred_element_type=F32)
        u = jnp.dot(x, wu_ref[...].astype(BF16), preferred_element_type=F32)
        h = (jax.nn.silu(g) * u).astype(BF16)
        o_ref[rows, :] += jnp.dot(h, wd_ref[...].astype(BF16), preferred_element_type=F32)

    n_occupied = (valid + sb - 1) // sb
    for n in range(1, N_SUB + 1):
        @pl.when(n_occupied == n)
        def _():
            mlp(slice(0, n * sb))


def _expert_mlp(te, tv, tb, xs, w_gate, w_up, w_down, layer, tg):
    n_tiles = te.shape[0]
    d, ff = w_gate.shape[-2:]
    nf = ff // TF
    fcol = lambda i, f, tv: jnp.where(tv[i] > 0, f, nf - 1)
    grid_spec = pltpu.PrefetchScalarGridSpec(
        num_scalar_prefetch=3,
        grid=(n_tiles, nf),
        in_specs=[
            pl.BlockSpec((tg, xs.shape[1]), lambda i, f, te, tv, tb: (tb[i], 0)),
            pl.BlockSpec((None, None, d, TF), lambda i, f, te, tv, tb: (layer, te[i], 0, fcol(i, f, tv))),
            pl.BlockSpec((None, None, d, TF), lambda i, f, te, tv, tb: (layer, te[i], 0, fcol(i, f, tv))),
            pl.BlockSpec((None, None, TF, d), lambda i, f, te, tv, tb: (layer, te[i], fcol(i, f, tv), 0)),
        ],
        out_specs=pl.BlockSpec((tg, d), lambda i, f, te, tv, tb: (i, 0)),
        scratch_shapes=[pltpu.VMEM((tg, d), BF16)],
    )
    return pl.pallas_call(
        _expert_mlp_kernel,
        grid_spec=grid_spec,
        out_shape=jax.ShapeDtypeStruct((n_tiles * tg, d), F32),
        compiler_params=_cparams("arbitrary", "arbitrary"),
        name="expert_mlp",
    )(te, tv, tb, xs, w_gate, w_up, w_down)


def _combine_ln_kernel(pa_ref, pb_ref, r_ref, ga_ref, gb_ref, g_ref, b_ref, y_ref, *rest, alpha, n_head_tiles):
    if n_head_tiles is None:
        oh_ref, ob_ref, ya_ref, yb_ref, sem_ref = rest
        ot_ref = None
    else:
        oh_ref, ot_ref, ob_ref, ya_ref, yb_ref, sem_ref = rest
    i = pl.program_id(0)
    n = pl.num_programs(0)
    slot = i % 2

    def fetch(tile, s):
        def body(r, c):
            t = tile * TC + r
            _row_copy(y_ref, ya_ref.at[s], pa_ref[t], r, sem_ref.at[s]).start()
            _row_copy(y_ref, yb_ref.at[s], pb_ref[t], r, sem_ref.at[s]).start()
            return c
        lax.fori_loop(0, TC, body, 0, unroll=DMA_UNROLL)

    @pl.when(i == 0)
    def _():
        fetch(0, 0)

    @pl.when(i + 1 < n)
    def _():
        fetch(i + 1, 1 - slot)

    pltpu.make_async_copy(y_ref.at[pl.ds(0, TC)], ya_ref.at[slot], sem_ref.at[slot]).wait()
    pltpu.make_async_copy(y_ref.at[pl.ds(0, TC)], yb_ref.at[slot], sem_ref.at[slot]).wait()
    moe = ga_ref[...] * ya_ref[slot] + gb_ref[...] * yb_ref[slot]
    y = _layer_norm_rows(alpha * r_ref[...] + moe, g_ref[...], b_ref[...])
    ob_ref[...] = y.astype(BF16)
    if n_head_tiles is None:
        oh_ref[...] = y
        return

    @pl.when(i < n_head_tiles)
    def _():
        oh_ref[...] = y

    @pl.when(i >= n_head_tiles)
    def _():
        ot_ref[...] = y


def _combine_ln(pos_a, pos_b, resid, gate_a, gate_b, g, b, y, alpha, m_head=None):
    mp, d = resid.shape
    row = lambda i, pa, pb: (i, 0)
    fixed = lambda i, pa, pb: (0, 0)
    if m_head is None:
        n_head_tiles = None
        f32_specs = [pl.BlockSpec((TC, d), row)]
        f32_shapes = [jax.ShapeDtypeStruct((mp, d), F32)]
    else:
        n_head_tiles = m_head // TC
        f32_specs = [pl.BlockSpec((TC, d), lambda i, pa, pb: (jnp.minimum(i, n_head_tiles - 1), 0)),
                     pl.BlockSpec((TC, d), lambda i, pa, pb: (jnp.maximum(i - n_head_tiles, 0), 0))]
        f32_shapes = [jax.ShapeDtypeStruct((m_head, d), F32), jax.ShapeDtypeStruct((mp - m_head, d), F32)]
    grid_spec = pltpu.PrefetchScalarGridSpec(
        num_scalar_prefetch=2,
        grid=(mp // TC,),
        in_specs=[
            pl.BlockSpec((TC, d), row),
            pl.BlockSpec((TC, 1), row),
            pl.BlockSpec((TC, 1), row),
            pl.BlockSpec((1, d), fixed),
            pl.BlockSpec((1, d), fixed),
            pl.BlockSpec(memory_space=pl.ANY),
        ],
        out_specs=f32_specs + [pl.BlockSpec((TC, d), row)],
        scratch_shapes=[pltpu.VMEM((2, TC, d), F32), pltpu.VMEM((2, TC, d), F32), pltpu.SemaphoreType.DMA((2,))],
    )
    outs = pl.pallas_call(
        functools.partial(_combine_ln_kernel, alpha=alpha, n_head_tiles=n_head_tiles),
        grid_spec=grid_spec,
        out_shape=f32_shapes + [jax.ShapeDtypeStruct((mp, d), BF16)],
        compiler_params=_cparams("arbitrary"),
        name="moe_combine_ln",
    )(pos_a, pos_b, resid, gate_a, gate_b, g, b, y)
    if m_head is None:
        return outs[0], outs[1]
    return (outs[0], outs[1]), outs[2]


def _expert_row_tile(m_real, ne):
    target = -(-(TOP_K * m_real * 11) // (ne * 10))
    n_split = max(1, (target + 512) // 1024)
    unit = SUBLANES * 2 * N_SUB
    return -(-target // (n_split * unit)) * unit


def _moe(x, xb, m_real, w_router, b_router, w_gate, w_up, w_down, layer, ln_g, ln_b, alpha, m_head=None):
    mp, d = x.shape
    ne = w_router.shape[-1]
    tg = _expert_row_tile(m_real, ne)
    comb, mask = _router(xb, w_router, b_router, layer)
    live = (jnp.arange(mp, dtype=jnp.int32) < m_real)[:, None]
    mask = jnp.where(live, mask, 0)
    col = jnp.arange(ne, dtype=jnp.int32)[None, :]
    rank = jnp.cumsum(mask, axis=0) - mask
    counts = jnp.sum(mask, axis=0)
    tiles_e = (counts + tg - 1) // tg
    tile_end = jnp.cumsum(tiles_e)
    tile_start = tile_end - tiles_e
    n_tiles = (TOP_K * m_real) // tg + ne
    p = n_tiles * tg
    pos = tile_start[None, :] * tg + rank
    e_hi = jnp.max(jnp.where(mask > 0, col, -1), axis=1, keepdims=True)
    e_lo = jnp.min(jnp.where(mask > 0, col, ne), axis=1, keepdims=True)
    pos_a = jnp.sum(jnp.where(col == e_hi, pos, 0), axis=1)
    pos_b = jnp.sum(jnp.where(col == e_lo, pos, 0), axis=1)
    gate_a = jnp.sum(jnp.where(col == e_hi, comb, 0.0), axis=1, keepdims=True)
    gate_b = jnp.sum(jnp.where(col == e_lo, comb, 0.0), axis=1, keepdims=True)
    spare = p + 2 * (jnp.arange(mp, dtype=jnp.int32) - m_real)
    send_a = jnp.where(live[:, 0], pos_a, spare)
    send_b = jnp.where(live[:, 0], pos_b, spare + 1)
    tile_id = jnp.arange(n_tiles, dtype=jnp.int32)
    n_active = tile_end[-1]
    tile_c = jnp.minimum(tile_id, jnp.maximum(n_active - 1, 0))
    te = jnp.minimum(jnp.sum((tile_end[None, :] <= tile_c[:, None]).astype(jnp.int32), axis=1), ne - 1)
    tv = jnp.clip(counts[te] - (tile_c - tile_start[te]) * tg, 0, tg)
    tv = jnp.where(tile_id < n_active, tv, 0).astype(jnp.int32)
    tb = tile_c.astype(jnp.int32)

    xs = _dispatch(send_a.astype(jnp.int32), send_b.astype(jnp.int32), x, p + 2 * (mp - m_real))
    y = _expert_mlp(te.astype(jnp.int32), tv, tb, xs, w_gate, w_up, w_down, layer, tg)
    return _combine_ln(pos_a.astype(jnp.int32), pos_b.astype(jnp.int32), x, gate_a, gate_b, ln_g, ln_b, y, alpha,
                       m_head)


def kernel(x_prompt, x_sample, cache_k, cache_v, cache_logf, state_conv, page_table, a_w_in, a_ln_g, a_ln_b, a_w_s, a_b_s, a_w_out, b_w_in, b_b_f, b_w_o, c_w_in, c_conv_w, c_w_out, ffn_w_gate, ffn_w_up, ffn_w_down, moe_w_router, moe_b_router, moe_w_gate, moe_w_up, moe_w_down, ln1_g, ln1_b, ln2_g, ln2_b):
    batch, seq, d = x_prompt.shape
    dec_batch, dec_seq, _ = x_sample.shape
    depth = ln1_g.shape[0]
    alpha = (2.0 * depth) ** 0.25
    m_prompt = batch * seq
    m_samp = dec_batch * dec_seq
    m_real = m_prompt + m_samp
    assert seq % TM == 0 and m_samp <= CHUNK and TM % CHUNK == 0 and dec_seq <= SUBLANES
    n_prompt_tiles = m_prompt // TM
    tiles_per_seq = seq // TM
    mp = m_prompt + TM
    n_pool = cache_k.shape[1]
    n_pages = page_table.shape[1]
    assert n_pages % PAGES_PER_STEP == 0 and n_pages % SUFFIX_PAGES_PER_STEP == 0
    dh = d // H_B
    e_a = a_w_out.shape[1]
    e_c = c_w_out.shape[1]
    g_a = e_a // H_A
    assert e_a == d and e_c == d and g_a % LANES == 0 and dh == LANES

    x = jnp.concatenate([x_prompt.reshape(m_prompt, d), x_sample.reshape(m_samp, d),
                         jnp.zeros((mp - m_real, d), F32)], axis=0)
    xb = x.astype(BF16)

    r = jnp.arange(TM)
    tri_prompt = r[:, None] >= r[None, :]
    tri_tail = tri_prompt & ((r[:, None] // dec_seq) == (r[None, :] // dec_seq))
    tri = jnp.stack([tri_prompt, tri_tail]).astype(BF16)
    rp = jnp.arange(PAGE_SIZE)
    tri_upper = (rp[None, :] > rp[:, None]).astype(BF16)

    a_w_out16 = a_w_out.astype(BF16)
    b_w_o16 = b_w_o.astype(BF16)
    c_w_out16 = c_w_out.astype(BF16)
    ffn_w_down16 = ffn_w_down.astype(BF16)

    k_p, v_p, lf_p, k_s, v_s, lf_s, conv_p, conv_s, chunk_v_s = [], [], [], [], [], [], [], [], []
    n_mix = 3
    for li in range(depth):
        kind = li % n_mix
        j = li // n_mix
        g1, b1 = ln1_g[li][None], ln1_b[li][None]
        g2, b2 = ln2_g[li][None], ln2_b[li][None]
        if kind == 0:
            nt = e_a // TN
            u, v = _proj(xb, [(a_w_in, j, 0), (a_w_in, j, nt)], _ep_gelu, [_tile_spec(), _tile_spec()],
                         [jax.ShapeDtypeStruct((mp, e_a), BF16), jax.ShapeDtypeStruct((mp, e_a), F32)],
                         n_col_tiles=nt, n_full_tiles=n_prompt_tiles, name="a_in")
            rc = jnp.arange(CHUNK)
            causal = rc[:, None] >= rc[None, :]
            w_prompt = jnp.where(causal[None], a_w_s[j], 0.0)
            same = (rc[:, None] // dec_seq) == (rc[None, :] // dec_seq)
            t_in = rc % dec_seq
            w_tail = jnp.where((causal & same)[None], a_w_s[j][:, t_in][:, :, t_in], 0.0)
            wmix = jnp.stack([w_prompt, w_tail]).astype(BF16)
            bias_p = jnp.repeat(a_b_s[j].T, g_a, axis=1)
            bias_t = jnp.repeat(a_b_s[j][:, t_in].T, g_a, axis=1)
            bias = jnp.stack([bias_p, bias_t])
            us, vs = _spatial(u, v, a_ln_g[j][None], a_ln_b[j][None], wmix, bias, n_prompt_tiles)
            chunk_v_s.append(vs[:m_samp].reshape(dec_batch, dec_seq, e_a))
            x, xb = _out_ln(us, a_w_out16, j, x, g1, b1, alpha, TM, m_prompt)
        elif kind == 1:
            nt = d // TN
            head_spec = pl.BlockSpec((TM, TN), lambda jj, i: (jnp.minimum(i, n_prompt_tiles - 1), jj))
            tail_spec = pl.BlockSpec((TM, TN), lambda jj, i: (jnp.maximum(i - n_prompt_tiles, 0), jj))
            full16 = jax.ShapeDtypeStruct((mp, d), BF16)
            head32 = jax.ShapeDtypeStruct((m_prompt, d), F32)
            tail32 = jax.ShapeDtypeStruct((mp - m_prompt, d), F32)
            q16, k16, v16, k32p, k32t, v32p, v32t = _proj(
                xb, [(b_w_in, j, 0), (b_w_in, j, nt), (b_w_in, j, 2 * nt)],
                _ep_qkv,
                [_tile_spec(), _tile_spec(), _tile_spec(), head_spec, tail_spec, head_spec, tail_spec],
                [full16, full16, full16, head32, tail32, head32, tail32], n_col_tiles=nt,
                n_full_tiles=n_prompt_tiles, name="b_in_qkv")
            w_f = b_w_in[j][:, 3 * d:]
            lf, c, c_lanes = _logf(xb, w_f, b_b_f[j][None, :], tri, n_prompt_tiles, tiles_per_seq)
            scale = dh ** -0.5
            c_t = c.T
            o_prompt = _fox_prompt(q16, k16, v16[:m_prompt].T, c_t[:, None, :m_prompt], c_lanes, batch, seq, scale)

            cache_k_rows = cache_k.reshape(-1, dh)
            cache_v_rows = cache_v.reshape(-1, dh)
            cache_lf_rows = cache_logf.reshape(-1, H_B)
            ck_past = _suffix(page_table, cache_lf_rows, j, n_pool, tri_upper)
            qpad = ((0, 0), (0, 0), (0, Q_ROWS - dec_seq), (0, 0))
            q_s = q16[m_prompt:m_real].reshape(dec_batch, dec_seq, H_B, dh).transpose(0, 2, 1, 3)
            q_s = jnp.pad(q_s, qpad).reshape(dec_batch, H_B * Q_ROWS, dh)
            c_s = c[m_prompt:m_real].reshape(dec_batch, dec_seq, H_B, 1).transpose(0, 2, 1, 3)
            cq_s = jnp.pad(c_s, qpad).reshape(dec_batch, H_B * Q_ROWS, 1)
            o_samp = _fox_decode(page_table, q_s, cq_s, k32t, v32t, c_t[:, m_prompt:m_prompt + CHUNK], ck_past,
                                 cache_k_rows, cache_v_rows, j, n_pool, scale, dec_seq)
            o_samp = o_samp.reshape(dec_batch, H_B, Q_ROWS, dh)[:, :, :dec_seq].transpose(0, 2, 1, 3)
            o_samp = o_samp.reshape(m_samp, d).astype(BF16)
            o_all = jnp.concatenate([o_prompt, o_samp, jnp.zeros((mp - m_real, d), BF16)], axis=0)
            x, xb = _out_ln(o_all, b_w_o16, j, x, g1, b1, alpha, TM, m_prompt)
            k_p.append(k32p.reshape(batch, seq, H_B, dh))
            v_p.append(v32p.reshape(batch, seq, H_B, dh))
            lf_p.append(lf[:m_prompt].reshape(batch, seq, H_B))
            k_s.append(k32t[:m_samp].reshape(dec_batch, dec_seq, H_B, dh))
            v_s.append(v32t[:m_samp].reshape(dec_batch, dec_seq, H_B, dh))
            lf_s.append(lf[m_prompt:m_real].reshape(dec_batch, dec_seq, H_B))
        else:
            nt = e_c // TN
            st = state_conv[j]
            p1 = jnp.zeros((dec_batch, dec_seq, e_c), F32).at[:, 0].set(st[:, 1])
            p2 = jnp.zeros((dec_batch, dec_seq, e_c), F32).at[:, 0].set(st[:, 0]).at[:, 1].set(st[:, 1])
            pad = jnp.zeros((TM - m_samp, e_c), F32)
            p1 = jnp.concatenate([p1.reshape(m_samp, e_c), pad], axis=0)
            p2 = jnp.concatenate([p2.reshape(m_samp, e_c), pad], axis=0)
            n_tiles = mp // TM
            ep = functools.partial(_ep_conv, tiles_per_seq=tiles_per_seq, dec_seq=dec_seq)
            bgy, ztail, zs = _proj(
                xb, [(c_w_in, j, 0), (c_w_in, j, nt), (c_w_in, j, 2 * nt)], ep,
                [_tile_spec(), pl.BlockSpec((SUBLANES, TN), lambda jj, i: (i, jj)),
                 pl.BlockSpec((TM, TN), lambda jj, i: (0, jj))],
                [jax.ShapeDtypeStruct((mp, e_c), BF16), jax.ShapeDtypeStruct((n_tiles * SUBLANES, e_c), F32),
                 jax.ShapeDtypeStruct((TM, e_c), F32)],
                n_col_tiles=nt, n_full_tiles=n_prompt_tiles, name="c_in_conv",
                extra_in=[c_conv_w, p1, p2],
                extra_specs=[pl.BlockSpec((None, CONV_W, TN), lambda jj, i: (j, 0, jj)),
                             pl.BlockSpec((TM, TN), lambda jj, i: (0, jj)),
                             pl.BlockSpec((TM, TN), lambda jj, i: (0, jj))],
                extra_scratch=[pltpu.VMEM((SUBLANES, TN), F32)])
            zt = ztail.reshape(n_tiles, SUBLANES, e_c)
            last_tiles = jnp.arange(batch) * tiles_per_seq + tiles_per_seq - 1
            conv_p.append(zt[last_tiles][:, SUBLANES - (CONV_W - 1):, :])
            conv_s.append(zs[:m_samp].reshape(dec_batch, dec_seq, e_c)[:, dec_seq - (CONV_W - 1):, :])
            x, xb = _out_ln(bgy, c_w_out16, j, x, g1, b1, alpha, TM, m_prompt)

        f = li // 2
        if li % 2 == 0:
            nt = ffn_w_gate.shape[-1] // TN
            (h,) = _proj(xb, [(ffn_w_gate, f, 0), (ffn_w_up, f, 0)], _ep_swiglu, [_tile_spec()],
                         [jax.ShapeDtypeStruct((mp, ffn_w_gate.shape[-1]), BF16)], n_col_tiles=nt,
                         n_full_tiles=n_prompt_tiles, name="ffn_up")
            x, xb = _out_ln(h, ffn_w_down16, f, x, g2, b2, alpha, TM_WIDE_K, m_prompt)
        else:
            last = li == depth - 1
            x, xb = _moe(x, xb, m_real, moe_w_router, moe_b_router[f][None], moe_w_gate, moe_w_up, moe_w_down, f,
                         g2, b2, alpha, m_prompt if last else None)

    if isinstance(x, tuple):
        x_head, x_tail = x
    else:
        x_head, x_tail = x[:m_prompt], x[m_prompt:]
    y_prompt = x_head.reshape(batch, seq, d)
    y_sample = x_tail[:m_samp].reshape(dec_batch, dec_seq, d)
    return (y_prompt, y_sample, jnp.stack(k_p), jnp.stack(v_p), jnp.stack(lf_p), jnp.stack(k_s),
            jnp.stack(v_s), jnp.stack(lf_s), jnp.stack(conv_p), jnp.stack(conv_s), jnp.stack(chunk_v_s))
```

```python
import functools

import jax
import jax.numpy as jnp
from jax import lax
from jax.experimental import pallas as pl
from jax.experimental.pallas import tpu as pltpu

F32 = jnp.float32
BF16 = jnp.bfloat16
U32 = jnp.uint32

H_A = 16
CHUNK = 128
H_B = 16
PAGE_SIZE = 128
CONV_W = 3
TOP_K = 2
LN_EPS = 1e-5
NEG_INF = -1e30

SUBLANES = 8
LANES = 128

TM = 512
TAIL_ROWS = 128
TM_WIDE = 1024
TN = 512
TM_WIDE_K = 256
TQ = 512
TKV = 512
HEADS_PER_STEP = 2
PAGES_PER_STEP = 4
SUFFIX_PAGES_PER_STEP = 8
Q_ROWS = 16
TF = 256
N_SUB = 8
TC = 256
DMA_UNROLL = 8

VMEM_LIMIT = 56 * 1024 * 1024


def _cparams(*sem):
    return pltpu.CompilerParams(dimension_semantics=sem, vmem_limit_bytes=VMEM_LIMIT)


def _layer_norm_rows(y, g, b):
    mu = jnp.mean(y, axis=-1, keepdims=True)
    d = y - mu
    var = jnp.mean(d * d, axis=-1, keepdims=True)
    return d * lax.rsqrt(var + LN_EPS) * g + b


def _proj_kernel(*refs, n_w, n_extra, n_out, epilogue, n_full_tiles):
    x_ref = refs[0]
    w_refs = refs[1:1 + n_w]
    extra = refs[1 + n_w:1 + n_w + n_extra]
    outs = refs[1 + n_w + n_extra:1 + n_w + n_extra + n_out]
    scratch = refs[1 + n_w + n_extra + n_out:]
    wb_refs = scratch[:n_w]
    rest = scratch[n_w:]
    i = pl.program_id(1)

    @pl.when(i == 0)
    def _():
        for w_ref, wb_ref in zip(w_refs, wb_refs):
            wb_ref[...] = w_ref[...].astype(BF16)

    def run(rows):
        x = x_ref[0:rows, :]
        accs = [jnp.dot(x, wb_ref[...], preferred_element_type=F32) for wb_ref in wb_refs]
        epilogue(accs, extra, outs, rest, rows)

    @pl.when(i < n_full_tiles)
    def _():
        run(x_ref.shape[0])

    @pl.when(i >= n_full_tiles)
    def _():
        run(TAIL_ROWS)


def _store_rows(o_ref, val, rows):
    o_ref[0:rows, :] = val.astype(o_ref.dtype)
    if rows < o_ref.shape[0]:
        o_ref[rows:, :] = jnp.zeros((o_ref.shape[0] - rows, o_ref.shape[1]), o_ref.dtype)


def _proj(x, weights, epilogue, out_specs, out_shapes, *, n_col_tiles, m_full, name, extra_in=(),
          extra_specs=(), extra_scratch=(), tn=TN, tm=TM):
    mp, k = x.shape
    n_w = len(weights)
    n_full_tiles = m_full // tm
    in_specs = [pl.BlockSpec((tm, k), lambda j, i: (i, 0))]
    args = [x]
    for w, layer, off in weights:
        in_specs.append(pl.BlockSpec((None, k, tn), functools.partial(
            lambda j, i, layer, off: (layer, 0, j + off), layer=layer, off=off)))
        args.append(w)
    in_specs += list(extra_specs)
    args += list(extra_in)
    kern = functools.partial(_proj_kernel, n_w=n_w, n_extra=len(extra_in), n_out=len(out_shapes),
                             epilogue=epilogue, n_full_tiles=n_full_tiles)
    return pl.pallas_call(
        kern,
        grid=(n_col_tiles, pl.cdiv(mp, tm)),
        in_specs=in_specs,
        out_specs=out_specs,
        out_shape=out_shapes,
        scratch_shapes=[pltpu.VMEM((k, tn), BF16) for _ in range(n_w)] + list(extra_scratch),
        compiler_params=_cparams("arbitrary", "arbitrary"),
        name=name,
    )(*args)


def _tile_spec(tn=TN, tm=TM):
    return pl.BlockSpec((tm, tn), lambda j, i: (i, j))


def _ep_gelu(accs, extra, outs, rest, rows):
    for acc, o in zip(accs, outs):
        _store_rows(o, jax.nn.gelu(acc), rows)


def _ep_qkv(accs, extra, outs, rest, rows):
    q16, k16, vt16, k32p, k32t, v32p, v32t = outs
    _store_rows(q16, accs[0], rows)
    _store_rows(k16, accs[1], rows)
    if rows == TM:
        vt16[...] = accs[2].T.astype(BF16)
        k32p[...] = accs[1]
        v32p[...] = accs[2]
    else:
        _store_rows(k32t, accs[1], rows)
        _store_rows(v32t, accs[2], rows)


def _ep_swiglu(accs, extra, outs, rest, rows):
    _store_rows(outs[0], jax.nn.silu(accs[0]) * accs[1], rows)


def _ep_conv(accs, extra, outs, rest, rows, *, tiles_per_seq, dec_seq):
    cw_ref, p1_ref, p2_ref = extra
    y_ref, ztail_ref, zs_ref = outs
    (carry_ref,) = rest
    i = pl.program_id(1)
    z = accs[1] * accs[2]
    row = lax.broadcasted_iota(jnp.int32, z.shape, 0)
    r1 = pltpu.roll(z, 1, 0)
    r2 = pltpu.roll(z, 2, 0)
    w0 = cw_ref[0:1, :]
    w1 = cw_ref[1:2, :]
    w2 = cw_ref[2:3, :]
    if rows == TM:
        @pl.when(i % tiles_per_seq == 0)
        def _():
            carry_ref[...] = jnp.zeros_like(carry_ref)

        c6 = carry_ref[6:7, :]
        c7 = carry_ref[7:8, :]
        s1 = jnp.where(row == 0, c7, r1)
        s2 = jnp.where(row == 0, c6, jnp.where(row == 1, c7, r2))
        y_ref[...] = (accs[0] * (w0 * s2 + w1 * s1 + w2 * z)).astype(y_ref.dtype)
        carry_ref[...] = z[TM - SUBLANES:, :]
    else:
        t = row % dec_seq
        s1 = jnp.where(t >= 1, r1, p1_ref[0:rows, :])
        s2 = jnp.where(t >= 2, r2, p2_ref[0:rows, :])
        _store_rows(y_ref, accs[0] * (w0 * s2 + w1 * s1 + w2 * z), rows)
        _store_rows(zs_ref, z, rows)
    ztail_ref[...] = z[rows - SUBLANES:, :]


def _out_ln_kernel(x_ref, w_ref, r_ref, g_ref, b_ref, o_ref, ob_ref, *, alpha, n_full_tiles):
    i = pl.program_id(0)

    def run(rows):
        acc = jnp.dot(x_ref[0:rows, :], w_ref[...], preferred_element_type=F32)
        y = _layer_norm_rows(alpha * r_ref[0:rows, :] + acc, g_ref[...], b_ref[...])
        _store_rows(o_ref, y, rows)
        _store_rows(ob_ref, y, rows)

    @pl.when(i < n_full_tiles)
    def _():
        run(o_ref.shape[0])

    @pl.when(i == n_full_tiles)
    def _():
        run(TAIL_ROWS)

    @pl.when(i > n_full_tiles)
    def _():
        o_ref[...] = jnp.zeros_like(o_ref)
        ob_ref[...] = jnp.zeros_like(ob_ref)


def _out_ln(x, w, layer, resid, g, b, alpha, tm, m_full):
    mp, k = x.shape
    d = resid.shape[1]
    row = lambda i: (i, 0)
    fixed = lambda i: (0, 0)
    return pl.pallas_call(
        functools.partial(_out_ln_kernel, alpha=alpha, n_full_tiles=m_full // tm),
        grid=(mp // tm,),
        in_specs=[
            pl.BlockSpec((tm, k), row),
            pl.BlockSpec((None, k, d), lambda i: (layer, 0, 0), pipeline_mode=pl.Buffered(1)),
            pl.BlockSpec((tm, d), row),
            pl.BlockSpec((1, d), fixed),
            pl.BlockSpec((1, d), fixed),
        ],
        out_specs=[pl.BlockSpec((tm, d), row), pl.BlockSpec((tm, d), row)],
        out_shape=[jax.ShapeDtypeStruct((mp, d), F32), jax.ShapeDtypeStruct((mp, d), BF16)],
        compiler_params=_cparams("arbitrary"),
        name="out_ln",
    )(x, w, resid, g, b)


def _spatial_kernel(u_ref, v_ref, g_ref, b_ref, wmix_ref, bias_ref, o_ref, vs_ref, vln_ref, *, n_prompt_tiles):
    i = pl.program_id(0)
    vln = _layer_norm_rows(v_ref[...], g_ref[...], b_ref[...])
    vln_ref[...] = vln.astype(BF16)

    @pl.when(i >= n_prompt_tiles)
    def _():
        vs_ref[...] = vln

    g_a = u_ref.shape[1] // H_A

    def chunk_body(c, carry):
        r0 = pl.multiple_of(c * CHUNK, CHUNK)
        for h in range(H_A):
            cols = slice(h * g_a, (h + 1) * g_a)
            s = jnp.dot(wmix_ref[h], vln_ref[pl.ds(r0, CHUNK), cols], preferred_element_type=F32)
            s = s + bias_ref[:, cols]
            o_ref[pl.ds(r0, CHUNK), cols] = (u_ref[pl.ds(r0, CHUNK), cols].astype(F32) * s).astype(o_ref.dtype)
        return carry

    lax.fori_loop(0, TM // CHUNK, chunk_body, 0)


def _spatial(u, v, ln_g, ln_b, wmix, bias, n_prompt_tiles):
    mp, e = u.shape
    sel = lambda i: jnp.where(i < n_prompt_tiles, 0, 1)
    return pl.pallas_call(
        functools.partial(_spatial_kernel, n_prompt_tiles=n_prompt_tiles),
        grid=(mp // TM,),
        in_specs=[
            pl.BlockSpec((TM, e), lambda i: (i, 0)),
            pl.BlockSpec((TM, e), lambda i: (i, 0)),
            pl.BlockSpec((1, e), lambda i: (0, 0)),
            pl.BlockSpec((1, e), lambda i: (0, 0)),
            pl.BlockSpec((None, H_A, CHUNK, CHUNK), lambda i: (sel(i), 0, 0, 0)),
            pl.BlockSpec((None, CHUNK, e), lambda i: (sel(i), 0, 0)),
        ],
        out_specs=[pl.BlockSpec((TM, e), lambda i: (i, 0)), pl.BlockSpec((TM, e), lambda i: (0, 0))],
        out_shape=[jax.ShapeDtypeStruct((mp, e), BF16), jax.ShapeDtypeStruct((TM, e), F32)],
        scratch_shapes=[pltpu.VMEM((TM, e), BF16)],
        compiler_params=_cparams("arbitrary"),
        name="spatial_mix",
    )(u, v, ln_g, ln_b, wmix, bias)


def _split3(x):
    hi = x.astype(BF16)
    r1 = x - hi.astype(F32)
    mid = r1.astype(BF16)
    lo = (r1 - mid.astype(F32)).astype(BF16)
    return hi, mid, lo


def _tri_matmul(tri, x):
    hi, mid, lo = _split3(x)
    out = jnp.dot(tri, lo, preferred_element_type=F32)
    out = out + jnp.dot(tri, mid, preferred_element_type=F32)
    return out + jnp.dot(tri, hi, preferred_element_type=F32)


def _logf_kernel(x_ref, w_ref, bf_ref, tri_ref, lf_ref, c_ref, cb_ref, carry_ref, *, tiles_per_seq):
    i = pl.program_id(0)

    @pl.when(i % tiles_per_seq == 0)
    def _():
        carry_ref[...] = jnp.zeros_like(carry_ref)

    z = jnp.dot(x_ref[...], w_ref[...].astype(BF16), preferred_element_type=F32) + bf_ref[...]
    lf = jax.nn.log_sigmoid(z)
    lf_ref[...] = lf
    c = _tri_matmul(tri_ref[...], lf) + carry_ref[0:1, :]
    c_ref[...] = c
    carry_ref[0:1, :] = c[TM - 1:TM, :]
    for h in range(c.shape[1]):
        cb_ref[h] = jnp.broadcast_to(c[:, h:h + 1], (TM, LANES))


def _logf(xb, w_f, b_f, tri, n_prompt_tiles, tiles_per_seq):
    mp, d = xb.shape
    h = w_f.shape[1]
    sel = lambda i: jnp.where(i < n_prompt_tiles, 0, 1)
    return pl.pallas_call(
        functools.partial(_logf_kernel, tiles_per_seq=tiles_per_seq),
        grid=(mp // TM,),
        in_specs=[
            pl.BlockSpec((TM, d), lambda i: (i, 0)),
            pl.BlockSpec((d, h), lambda i: (0, 0)),
            pl.BlockSpec((1, h), lambda i: (0, 0)),
            pl.BlockSpec((None, TM, TM), lambda i: (sel(i), 0, 0)),
        ],
        out_specs=[pl.BlockSpec((TM, h), lambda i: (i, 0)), pl.BlockSpec((TM, h), lambda i: (i, 0)),
                   pl.BlockSpec((h, TM, LANES), lambda i: (0, i, 0))],
        out_shape=[jax.ShapeDtypeStruct((mp, h), F32), jax.ShapeDtypeStruct((mp, h), F32),
                   jax.ShapeDtypeStruct((h, mp, LANES), F32)],
        scratch_shapes=[pltpu.VMEM((SUBLANES, h), F32)],
        compiler_params=_cparams("arbitrary"),
        name="logf_cumsum",
    )(xb, w_f, b_f, tri)


def _fox_prompt_kernel(q_ref, k_ref, vt_ref, cq_ref, ckb_ref, o_ref, *, scale):
    qi = pl.program_id(2)
    tq = q_ref.shape[0]
    dh = q_ref.shape[1] // HEADS_PER_STEP
    heads = range(HEADS_PER_STEP)
    qs = [q_ref[:, g * dh:(g + 1) * dh] for g in heads]
    cqs = [cq_ref[g] for g in heads]

    def step(g, kv0, carry, masked):
        m, l, acc = carry
        k = k_ref[pl.ds(kv0, TKV), g * dh:(g + 1) * dh]
        s = lax.dot_general(k, qs[g], (((1,), (1,)), ((), ())), preferred_element_type=F32)
        ckb = ckb_ref[g, pl.ds(kv0, TKV), :]
        s = s * scale + cqs[g] - jnp.concatenate([ckb] * (tq // LANES), axis=1)
        if masked:
            key = lax.broadcasted_iota(jnp.int32, s.shape, 0)
            qrow = lax.broadcasted_iota(jnp.int32, s.shape, 1)
            s = jnp.where(qrow >= key, s, NEG_INF)
        m_new = jnp.maximum(m, jnp.max(s, axis=0, keepdims=True))
        a = jnp.exp(m - m_new)
        p = jnp.exp(s - m_new)
        l = a * l + jnp.sum(p, axis=0, keepdims=True)
        vt = vt_ref[g * dh:(g + 1) * dh, pl.ds(kv0, TKV)]
        acc = a * acc + jnp.dot(vt, p.astype(BF16), preferred_element_type=F32)
        return m_new, l, acc

    init = tuple((jnp.full((1, tq), NEG_INF, F32), jnp.zeros((1, tq), F32), jnp.zeros((dh, tq), F32))
                 for _ in heads)

    def body(j, carries):
        kv0 = pl.multiple_of(j * TKV, TKV)
        return tuple(step(g, kv0, carries[g], False) for g in heads)

    carries = lax.fori_loop(0, qi * (tq // TKV), body, init)
    for g in heads:
        m, l, acc = step(g, pl.multiple_of(qi * tq, tq), carries[g], True)
        o_ref[:, g * dh:(g + 1) * dh] = (acc / l).T.astype(o_ref.dtype)


def _fox_prompt(q, k, vt, cq, ckb, batch, seq, scale):
    dh = q.shape[1] // H_B
    nq = seq // TQ
    w = HEADS_PER_STEP * dh
    return pl.pallas_call(
        functools.partial(_fox_prompt_kernel, scale=scale),
        grid=(batch, H_B // HEADS_PER_STEP, nq),
        in_specs=[
            pl.BlockSpec((TQ, w), lambda b, h, i: (b * nq + i, h)),
            pl.BlockSpec((seq, w), lambda b, h, i: (b, h)),
            pl.BlockSpec((w, seq), lambda b, h, i: (h, b)),
            pl.BlockSpec((HEADS_PER_STEP, 1, TQ), lambda b, h, i: (h, 0, b * nq + i)),
            pl.BlockSpec((HEADS_PER_STEP, seq, LANES), lambda b, h, i: (h, b, 0)),
        ],
        out_specs=pl.BlockSpec((TQ, w), lambda b, h, i: (b * nq + i, h)),
        out_shape=jax.ShapeDtypeStruct((batch * seq, H_B * dh), BF16),
        compiler_params=_cparams("arbitrary", "arbitrary", "arbitrary"),
        name="fox_prompt",
    )(q, k, vt, cq, ckb)


def _suffix_kernel(pt_ref, *refs):
    sp = SUFFIX_PAGES_PER_STEP
    lf_refs = refs[:sp]
    tri_ref, o_ref, carry_ref = refs[sp:]

    @pl.when(pl.program_id(1) == 0)
    def _():
        carry_ref[...] = jnp.zeros_like(carry_ref)

    carry = carry_ref[...]
    tri = tri_ref[...]
    ones = jnp.ones((PAGE_SIZE, LANES), BF16)
    for r in reversed(range(sp)):
        within = jnp.zeros(carry.shape, F32)
        total = jnp.zeros(carry.shape, F32)
        for part in reversed(_split3(lf_refs[r][...])):
            within = within + lax.dot_general(part, tri, (((0,), (1,)), ((), ())), preferred_element_type=F32)
            total = total + lax.dot_general(part, ones, (((0,), (0,)), ((), ())), preferred_element_type=F32)
        o_ref[:, r * PAGE_SIZE:(r + 1) * PAGE_SIZE] = -(within + carry)
        carry = carry + total
    carry_ref[...] = carry


def _suffix(page_table, cache_lf_rows, layer, n_pool, tri_upper):
    n, n_pages = page_table.shape
    h = cache_lf_rows.shape[-1]
    sp = SUFFIX_PAGES_PER_STEP
    groups = n_pages // sp

    def page_spec(r):
        return pl.BlockSpec((PAGE_SIZE, h),
                            lambda b, p, pt: (layer * n_pool + pt[b, (groups - 1 - p) * sp + r], 0))

    grid_spec = pltpu.PrefetchScalarGridSpec(
        num_scalar_prefetch=1,
        grid=(n, groups),
        in_specs=[page_spec(r) for r in range(sp)] + [
            pl.BlockSpec((PAGE_SIZE, PAGE_SIZE), lambda b, p, pt: (0, 0))],
        out_specs=pl.BlockSpec((None, h, sp * PAGE_SIZE), lambda b, p, pt: (b, 0, groups - 1 - p)),
        scratch_shapes=[pltpu.VMEM((h, LANES), F32)],
    )
    return pl.pallas_call(
        _suffix_kernel,
        grid_spec=grid_spec,
        out_shape=jax.ShapeDtypeStruct((n, h, n_pages * PAGE_SIZE), F32),
        compiler_params=_cparams("arbitrary", "arbitrary"),
        name="logf_suffix",
    )(page_table, *([cache_lf_rows] * sp), tri_upper)


def _fox_decode_kernel(pt_ref, q_ref, cq_ref, knew_ref, vnew_ref, cknew_ref, ckpast_ref, *rest, scale, dec_seq):
    pp = PAGES_PER_STEP
    kp_refs = rest[:pp]
    vp_refs = rest[pp:2 * pp]
    o_ref = rest[2 * pp]
    s_ref, p_ref, m_ref, l_ref, acc_ref = rest[2 * pp + 1:]
    n = pl.program_id(0)
    pg = pl.program_id(1)
    dh = q_ref.shape[1]
    nt_dims = (((1,), (1,)), ((), ()))

    def hrows(h):
        return slice(h * Q_ROWS, (h + 1) * Q_ROWS)

    def softmax_update(width):
        s = s_ref[:, :width]
        m_old = m_ref[...]
        m_new = jnp.maximum(m_old, jnp.max(s, axis=-1, keepdims=True))
        a = jnp.exp(m_old - m_new)
        p = jnp.exp(s - m_new)
        l_ref[...] = a * l_ref[...] + jnp.sum(p, axis=-1, keepdims=True)
        m_ref[...] = m_new
        p_ref[:, :width] = p.astype(BF16)
        acc_ref[...] = a * acc_ref[...]

    @pl.when(pg == 0)
    def _():
        m_ref[...] = jnp.full(m_ref.shape, NEG_INF, F32)
        l_ref[...] = jnp.zeros_like(l_ref)
        acc_ref[...] = jnp.zeros_like(acc_ref)
        nk = knew_ref.shape[0]
        t = lax.broadcasted_iota(jnp.int32, (Q_ROWS, nk), 0)
        key = lax.broadcasted_iota(jnp.int32, (Q_ROWS, nk), 1)
        mask = (key >= n * dec_seq) & (key <= n * dec_seq + t)
        for h in range(H_B):
            k_h = knew_ref[:, h * dh:(h + 1) * dh].astype(BF16)
            s = lax.dot_general(q_ref[hrows(h), :], k_h, nt_dims, preferred_element_type=F32)
            s = s * scale + cq_ref[hrows(h), :] - cknew_ref[h:h + 1, :]
            s_ref[hrows(h), 0:nk] = jnp.where(mask, s, NEG_INF)
        softmax_update(nk)
        for h in range(H_B):
            v_h = vnew_ref[:, h * dh:(h + 1) * dh].astype(BF16)
            acc_ref[hrows(h), :] += jnp.dot(p_ref[hrows(h), 0:nk], v_h, preferred_element_type=F32)

    for r in range(pp):
        cols = slice(r * PAGE_SIZE, (r + 1) * PAGE_SIZE)
        for h in range(H_B):
            k_h = kp_refs[r][pl.ds(h, PAGE_SIZE, stride=H_B), :].astype(BF16)
            s = lax.dot_general(q_ref[hrows(h), :], k_h, nt_dims, preferred_element_type=F32)
            s_ref[hrows(h), cols] = s * scale + cq_ref[hrows(h), :] - ckpast_ref[h:h + 1, cols]
    softmax_update(pp * PAGE_SIZE)
    for h in range(H_B):
        pv = None
        for r in range(pp):
            cols = slice(r * PAGE_SIZE, (r + 1) * PAGE_SIZE)
            v_h = vp_refs[r][pl.ds(h, PAGE_SIZE, stride=H_B), :].astype(BF16)
            d = jnp.dot(p_ref[hrows(h), cols], v_h, preferred_element_type=F32)
            pv = d if pv is None else pv + d
        acc_ref[hrows(h), :] += pv

    @pl.when(pg == pl.num_programs(1) - 1)
    def _():
        o_ref[...] = acc_ref[...] / l_ref[...]


def _fox_decode(page_table, q_s, cq_s, k_new, v_new, ck_new, ck_past, cache_k_rows, cache_v_rows, layer, n_pool,
                scale, dec_seq):
    n, n_pages = page_table.shape
    rows, dh = q_s.shape[1:]
    d = k_new.shape[1]
    nk = ck_new.shape[1]
    pp = PAGES_PER_STEP
    steps = n_pages // pp
    page_rows = PAGE_SIZE * H_B

    def page_spec(r):
        return pl.BlockSpec((page_rows, dh), lambda b, p, pt: (layer * n_pool + pt[b, p * pp + r], 0))

    in_specs = [
        pl.BlockSpec((None, rows, dh), lambda b, p, pt: (b, 0, 0)),
        pl.BlockSpec((None, rows, 1), lambda b, p, pt: (b, 0, 0)),
        pl.BlockSpec((nk, d), lambda b, p, pt: (0, 0)),
        pl.BlockSpec((nk, d), lambda b, p, pt: (0, 0)),
        pl.BlockSpec((H_B, nk), lambda b, p, pt: (0, 0)),
        pl.BlockSpec((None, H_B, pp * PAGE_SIZE), lambda b, p, pt: (b, 0, p)),
    ]
    in_specs += [page_spec(r) for r in range(pp)]
    in_specs += [page_spec(r) for r in range(pp)]
    grid_spec = pltpu.PrefetchScalarGridSpec(
        num_scalar_prefetch=1,
        grid=(n, steps),
        in_specs=in_specs,
        out_specs=pl.BlockSpec((None, rows, dh), lambda b, p, pt: (b, 0, 0)),
        scratch_shapes=[pltpu.VMEM((rows, pp * PAGE_SIZE), F32), pltpu.VMEM((rows, pp * PAGE_SIZE), BF16),
                        pltpu.VMEM((rows, 1), F32), pltpu.VMEM((rows, 1), F32), pltpu.VMEM((rows, dh), F32)],
    )
    return pl.pallas_call(
        functools.partial(_fox_decode_kernel, scale=scale, dec_seq=dec_seq),
        grid_spec=grid_spec,
        out_shape=jax.ShapeDtypeStruct((n, rows, dh), F32),
        compiler_params=_cparams("arbitrary", "arbitrary"),
        name="fox_decode",
    )(page_table, q_s, cq_s, k_new, v_new, ck_new, ck_past,
      *([cache_k_rows] * pp), *([cache_v_rows] * pp))


def _router_kernel(x_ref, w_ref, b_ref, comb_ref, mask_ref):
    x = x_ref[...]
    w = w_ref[...]
    xh = x.astype(BF16)
    xl = (x - xh.astype(F32)).astype(BF16)
    wh = w.astype(BF16)
    wl = (w - wh.astype(F32)).astype(BF16)
    logits = jnp.dot(xh, wl, preferred_element_type=F32) + jnp.dot(xl, wh, preferred_element_type=F32)
    logits = logits + jnp.dot(xh, wh, preferred_element_type=F32) + b_ref[...]
    ne = logits.shape[1]
    col = lax.broadcasted_iota(jnp.int32, logits.shape, 1)
    m1 = jnp.max(logits, axis=-1, keepdims=True)
    i1 = jnp.min(jnp.where(logits == m1, col, ne), axis=-1, keepdims=True)
    rest = jnp.where(col == i1, -jnp.inf, logits)
    m2 = jnp.max(rest, axis=-1, keepdims=True)
    i2 = jnp.min(jnp.where(rest == m2, col, ne), axis=-1, keepdims=True)
    e2 = jnp.exp(m2 - m1)
    den = 1.0 + e2
    comb_ref[...] = jnp.where(col == i1, 1.0 / den, 0.0) + jnp.where(col == i2, e2 / den, 0.0)
    mask_ref[...] = ((col == i1) | (col == i2)).astype(jnp.int32)


def _router(x, w_r, b_r, layer):
    mp, d = x.shape
    ne = w_r.shape[-1]
    return pl.pallas_call(
        _router_kernel,
        grid=(mp // TM,),
        in_specs=[
            pl.BlockSpec((TM, d), lambda i: (i, 0)),
            pl.BlockSpec((None, d, ne), lambda i: (layer, 0, 0)),
            pl.BlockSpec((1, ne), lambda i: (0, 0)),
        ],
        out_specs=[pl.BlockSpec((TM, ne), lambda i: (i, 0)), pl.BlockSpec((TM, ne), lambda i: (i, 0))],
        out_shape=[jax.ShapeDtypeStruct((mp, ne), F32), jax.ShapeDtypeStruct((mp, ne), jnp.int32)],
        compiler_params=_cparams("arbitrary"),
        name="router",
    )(x, w_r, b_r)


def _row_copy(src, dst, src_row, dst_row, sem):
    return pltpu.make_async_copy(src.at[pl.ds(src_row, 1)], dst.at[pl.ds(dst_row, 1)], sem)


def _dispatch_kernel(pa_ref, pb_ref, x_ref, xs_in_ref, xs_ref, buf_ref, sem_ref):
    del xs_in_ref
    i = pl.program_id(0)
    slot = i % 2
    half = x_ref.shape[1] // 2
    buf = buf_ref.at[slot]
    sem = sem_ref.at[slot]

    def drain(b, s):
        for _ in range(2):
            pltpu.make_async_copy(b, xs_ref.at[pl.ds(0, TM)], s).wait()

    @pl.when(i >= 2)
    def _():
        drain(buf, sem)

    x = x_ref[...]
    hi = x[:, :half].astype(BF16).astype(F32)
    lo = x[:, half:].astype(BF16).astype(F32)
    buf[...] = pltpu.bitcast(hi, U32) | (pltpu.bitcast(lo, U32) >> 16)

    def send(r, c):
        t = i * TM + r
        _row_copy(buf, xs_ref, r, pa_ref[t], sem).start()
        _row_copy(buf, xs_ref, r, pb_ref[t], sem).start()
        return c

    lax.fori_loop(0, TM, send, 0, unroll=DMA_UNROLL)

    @pl.when(i == pl.num_programs(0) - 1)
    def _():
        drain(buf, sem)

        @pl.when(i >= 1)
        def _():
            drain(buf_ref.at[1 - slot], sem_ref.at[1 - slot])


def _dispatch(pos_a, pos_b, x, n_rows):
    mp, d = x.shape
    words = d // 2
    assert words % LANES == 0
    xs0 = jnp.zeros((n_rows, words), U32)
    grid_spec = pltpu.PrefetchScalarGridSpec(
        num_scalar_prefetch=2,
        grid=(mp // TM,),
        in_specs=[pl.BlockSpec((TM, d), lambda i, pa, pb: (i, 0)), pl.BlockSpec(memory_space=pl.ANY)],
        out_specs=pl.BlockSpec(memory_space=pl.ANY),
        scratch_shapes=[pltpu.VMEM((2, TM, words), U32), pltpu.SemaphoreType.DMA((2,))],
    )
    return pl.pallas_call(
        _dispatch_kernel,
        grid_spec=grid_spec,
        out_shape=jax.ShapeDtypeStruct(xs0.shape, U32),
        input_output_aliases={3: 0},
        compiler_params=_cparams("arbitrary"),
        name="moe_dispatch",
    )(pos_a, pos_b, x, xs0)


def _expert_mlp_kernel(te_ref, tv_ref, tb_ref, xs_ref, wg_ref, wu_ref, wd_ref, o_ref, xb_ref):
    i = pl.program_id(0)
    f = pl.program_id(1)
    valid = tv_ref[i]
    tg = o_ref.shape[0]
    sb = tg // N_SUB
    half = o_ref.shape[1] // 2

    @pl.when((f == 0) & (valid > 0))
    def _():
        u = xs_ref[...]
        xb_ref[:, :half] = pltpu.bitcast(u & jnp.uint32(0xFFFF0000), F32).astype(BF16)
        xb_ref[:, half:] = pltpu.bitcast(u << 16, F32).astype(BF16)
        o_ref[...] = jnp.zeros_like(o_ref)

    @pl.when((f == 0) & (valid == 0))
    def _():
        o_ref[...] = jnp.zeros_like(o_ref)

    def mlp(rows):
        x = xb_ref[rows, :]
        g = jnp.dot(x, wg_ref[...].astype(BF16), preferred_element_type=F32)
        u = jnp.dot(x, wu_ref[...].astype(BF16), preferred_element_type=F32)
        h = (jax.nn.silu(g) * u).astype(BF16)
        o_ref[rows, :] += jnp.dot(h, wd_ref[...].astype(BF16), preferred_element_type=F32)

    n_occupied = (valid + sb - 1) // sb
    for n in range(1, N_SUB + 1):
        @pl.when(n_occupied == n)
        def _():
            mlp(slice(0, n * sb))


def _expert_mlp(te, tv, tb, xs, w_gate, w_up, w_down, layer, tg):
    n_tiles = te.shape[0]
    d, ff = w_gate.shape[-2:]
    nf = ff // TF
    fcol = lambda i, f, tv: jnp.where(tv[i] > 0, f, nf - 1)
    grid_spec = pltpu.PrefetchScalarGridSpec(
        num_scalar_prefetch=3,
        grid=(n_tiles, nf),
        in_specs=[
            pl.BlockSpec((tg, xs.shape[1]), lambda i, f, te, tv, tb: (tb[i], 0)),
            pl.BlockSpec((None, None, d, TF), lambda i, f, te, tv, tb: (layer, te[i], 0, fcol(i, f, tv))),
            pl.BlockSpec((None, None, d, TF), lambda i, f, te, tv, tb: (layer, te[i], 0, fcol(i, f, tv))),
            pl.BlockSpec((None, None, TF, d), lambda i, f, te, tv, tb: (layer, te[i], fcol(i, f, tv), 0)),
        ],
        out_specs=pl.BlockSpec((tg, d), lambda i, f, te, tv, tb: (i, 0)),
        scratch_shapes=[pltpu.VMEM((tg, d), BF16)],
    )
    return pl.pallas_call(
        _expert_mlp_kernel,
        grid_spec=grid_spec,
        out_shape=jax.ShapeDtypeStruct((n_tiles * tg, d), F32),
        compiler_params=_cparams("arbitrary", "arbitrary"),
        name="expert_mlp",
    )(te, tv, tb, xs, w_gate, w_up, w_down)


def _combine_ln_kernel(pa_ref, pb_ref, r_ref, ga_ref, gb_ref, g_ref, b_ref, y_ref, *rest, alpha, n_head_tiles):
    if n_head_tiles is None:
        oh_ref, ob_ref, ya_ref, yb_ref, sem_ref = rest
        ot_ref = None
    else:
        oh_ref, ot_ref, ob_ref, ya_ref, yb_ref, sem_ref = rest
    i = pl.program_id(0)
    n = pl.num_programs(0)
    slot = i % 2

    def fetch(tile, s):
        def body(r, c):
            t = tile * TC + r
            _row_copy(y_ref, ya_ref.at[s], pa_ref[t], r, sem_ref.at[s]).start()
            _row_copy(y_ref, yb_ref.at[s], pb_ref[t], r, sem_ref.at[s]).start()
            return c
        lax.fori_loop(0, TC, body, 0, unroll=DMA_UNROLL)

    @pl.when(i == 0)
    def _():
        fetch(0, 0)

    @pl.when(i + 1 < n)
    def _():
        fetch(i + 1, 1 - slot)

    pltpu.make_async_copy(y_ref.at[pl.ds(0, TC)], ya_ref.at[slot], sem_ref.at[slot]).wait()
    pltpu.make_async_copy(y_ref.at[pl.ds(0, TC)], yb_ref.at[slot], sem_ref.at[slot]).wait()
    moe = ga_ref[...] * ya_ref[slot] + gb_ref[...] * yb_ref[slot]
    y = _layer_norm_rows(alpha * r_ref[...] + moe, g_ref[...], b_ref[...])
    ob_ref[...] = y.astype(BF16)
    if n_head_tiles is None:
        oh_ref[...] = y
        return

    @pl.when(i < n_head_tiles)
    def _():
        oh_ref[...] = y

    @pl.when(i >= n_head_tiles)
    def _():
        ot_ref[...] = y


def _combine_ln(pos_a, pos_b, resid, gate_a, gate_b, g, b, y, alpha, m_head=None):
    mp, d = resid.shape
    row = lambda i, pa, pb: (i, 0)
    fixed = lambda i, pa, pb: (0, 0)
    if m_head is None:
        n_head_tiles = None
        f32_specs = [pl.BlockSpec((TC, d), row)]
        f32_shapes = [jax.ShapeDtypeStruct((mp, d), F32)]
    else:
        n_head_tiles = m_head // TC
        f32_specs = [pl.BlockSpec((TC, d), lambda i, pa, pb: (jnp.minimum(i, n_head_tiles - 1), 0)),
                     pl.BlockSpec((TC, d), lambda i, pa, pb: (jnp.maximum(i - n_head_tiles, 0), 0))]
        f32_shapes = [jax.ShapeDtypeStruct((m_head, d), F32), jax.ShapeDtypeStruct((mp - m_head, d), F32)]
    grid_spec = pltpu.PrefetchScalarGridSpec(
        num_scalar_prefetch=2,
        grid=(mp // TC,),
        in_specs=[
            pl.BlockSpec((TC, d), row),
            pl.BlockSpec((TC, 1), row),
            pl.BlockSpec((TC, 1), row),
            pl.BlockSpec((1, d), fixed),
            pl.BlockSpec((1, d), fixed),
            pl.BlockSpec(memory_space=pl.ANY),
        ],
        out_specs=f32_specs + [pl.BlockSpec((TC, d), row)],
        scratch_shapes=[pltpu.VMEM((2, TC, d), F32), pltpu.VMEM((2, TC, d), F32), pltpu.SemaphoreType.DMA((2,))],
    )
    outs = pl.pallas_call(
        functools.partial(_combine_ln_kernel, alpha=alpha, n_head_tiles=n_head_tiles),
        grid_spec=grid_spec,
        out_shape=f32_shapes + [jax.ShapeDtypeStruct((mp, d), BF16)],
        compiler_params=_cparams("arbitrary"),
        name="moe_combine_ln",
    )(pos_a, pos_b, resid, gate_a, gate_b, g, b, y)
    if m_head is None:
        return outs[0], outs[1]
    return (outs[0], outs[1]), outs[2]


def _expert_row_tile(m_real, ne):
    target = -(-(TOP_K * m_real * 11) // (ne * 10))
    n_split = max(1, (target + 512) // 1024)
    unit = SUBLANES * 2 * N_SUB
    return -(-target // (n_split * unit)) * unit


def _moe(x, m_real, w_router, b_router, w_gate, w_up, w_down, layer, ln_g, ln_b, alpha, m_head=None):
    mp, d = x.shape
    ne = w_router.shape[-1]
    tg = _expert_row_tile(m_real, ne)
    comb, mask = _router(x, w_router, b_router, layer)
    live = (jnp.arange(mp, dtype=jnp.int32) < m_real)[:, None]
    mask = jnp.where(live, mask, 0)
    col = jnp.arange(ne, dtype=jnp.int32)[None, :]
    rank = jnp.cumsum(mask, axis=0) - mask
    counts = jnp.sum(mask, axis=0)
    tiles_e = (counts + tg - 1) // tg
    tile_end = jnp.cumsum(tiles_e)
    tile_start = tile_end - tiles_e
    n_tiles = (TOP_K * m_real) // tg + ne
    p = n_tiles * tg
    pos = tile_start[None, :] * tg + rank
    e_hi = jnp.max(jnp.where(mask > 0, col, -1), axis=1, keepdims=True)
    e_lo = jnp.min(jnp.where(mask > 0, col, ne), axis=1, keepdims=True)
    pos_a = jnp.sum(jnp.where(col == e_hi, pos, 0), axis=1)
    pos_b = jnp.sum(jnp.where(col == e_lo, pos, 0), axis=1)
    gate_a = jnp.sum(jnp.where(col == e_hi, comb, 0.0), axis=1, keepdims=True)
    gate_b = jnp.sum(jnp.where(col == e_lo, comb, 0.0), axis=1, keepdims=True)
    spare = p + 2 * (jnp.arange(mp, dtype=jnp.int32) - m_real)
    send_a = jnp.where(live[:, 0], pos_a, spare)
    send_b = jnp.where(live[:, 0], pos_b, spare + 1)
    tile_id = jnp.arange(n_tiles, dtype=jnp.int32)
    n_active = tile_end[-1]
    tile_c = jnp.minimum(tile_id, jnp.maximum(n_active - 1, 0))
    te = jnp.minimum(jnp.sum((tile_end[None, :] <= tile_c[:, None]).astype(jnp.int32), axis=1), ne - 1)
    tv = jnp.clip(counts[te] - (tile_c - tile_start[te]) * tg, 0, tg)
    tv = jnp.where(tile_id < n_active, tv, 0).astype(jnp.int32)
    tb = tile_c.astype(jnp.int32)

    xs = _dispatch(send_a.astype(jnp.int32), send_b.astype(jnp.int32), x, p + 2 * (mp - m_real))
    y = _expert_mlp(te.astype(jnp.int32), tv, tb, xs, w_gate, w_up, w_down, layer, tg)
    return _combine_ln(pos_a.astype(jnp.int32), pos_b.astype(jnp.int32), x, gate_a, gate_b, ln_g, ln_b, y, alpha,
                       m_head)


def kernel(x_prompt, x_sample, cache_k, cache_v, cache_logf, state_conv, page_table, a_w_in, a_ln_g, a_ln_b, a_w_s, a_b_s, a_w_out, b_w_in, b_b_f, b_w_o, c_w_in, c_conv_w, c_w_out, ffn_w_gate, ffn_w_up, ffn_w_down, moe_w_router, moe_b_router, moe_w_gate, moe_w_up, moe_w_down, ln1_g, ln1_b, ln2_g, ln2_b):
    batch, seq, d = x_prompt.shape
    dec_batch, dec_seq, _ = x_sample.shape
    depth = ln1_g.shape[0]
    alpha = (2.0 * depth) ** 0.25
    m_prompt = batch * seq
    m_samp = dec_batch * dec_seq
    m_real = m_prompt + m_samp
    assert seq % TM == 0 and m_samp <= CHUNK and TM % CHUNK == 0 and dec_seq <= SUBLANES
    n_prompt_tiles = m_prompt // TM
    tiles_per_seq = seq // TM
    mp = m_prompt + TM
    n_pool = cache_k.shape[1]
    n_pages = page_table.shape[1]
    assert n_pages % PAGES_PER_STEP == 0 and n_pages % SUFFIX_PAGES_PER_STEP == 0
    dh = d // H_B
    e_a = a_w_out.shape[1]
    e_c = c_w_out.shape[1]
    g_a = e_a // H_A
    assert e_a == d and e_c == d and g_a % LANES == 0 and dh == LANES

    x = jnp.concatenate([x_prompt.reshape(m_prompt, d), x_sample.reshape(m_samp, d),
                         jnp.zeros((mp - m_real, d), F32)], axis=0)
    xb = x.astype(BF16)

    r = jnp.arange(TM)
    tri_prompt = r[:, None] >= r[None, :]
    tri_tail = tri_prompt & ((r[:, None] // dec_seq) == (r[None, :] // dec_seq))
    tri = jnp.stack([tri_prompt, tri_tail]).astype(BF16)
    rp = jnp.arange(PAGE_SIZE)
    tri_upper = (rp[None, :] > rp[:, None]).astype(BF16)

    a_w_out16 = a_w_out.astype(BF16)
    b_w_o16 = b_w_o.astype(BF16)
    c_w_out16 = c_w_out.astype(BF16)
    ffn_w_down16 = ffn_w_down.astype(BF16)

    k_p, v_p, lf_p, k_s, v_s, lf_s, conv_p, conv_s, chunk_v_s = [], [], [], [], [], [], [], [], []
    n_mix = 3
    for li in range(depth):
        kind = li % n_mix
        j = li // n_mix
        g1, b1 = ln1_g[li][None], ln1_b[li][None]
        g2, b2 = ln2_g[li][None], ln2_b[li][None]
        if kind == 0:
            nt = e_a // TN
            u, v = _proj(xb, [(a_w_in, j, 0), (a_w_in, j, nt)], _ep_gelu,
                         [_tile_spec(tm=TM_WIDE), _tile_spec(tm=TM_WIDE)],
                         [jax.ShapeDtypeStruct((mp, e_a), BF16), jax.ShapeDtypeStruct((mp, e_a), F32)],
                         n_col_tiles=nt, m_full=m_prompt, name="a_in", tm=TM_WIDE)
            rc = jnp.arange(CHUNK)
            causal = rc[:, None] >= rc[None, :]
            w_prompt = jnp.where(causal[None], a_w_s[j], 0.0)
            same = (rc[:, None] // dec_seq) == (rc[None, :] // dec_seq)
            t_in = rc % dec_seq
            w_tail = jnp.where((causal & same)[None], a_w_s[j][:, t_in][:, :, t_in], 0.0)
            wmix = jnp.stack([w_prompt, w_tail]).astype(BF16)
            bias_p = jnp.repeat(a_b_s[j].T, g_a, axis=1)
            bias_t = jnp.repeat(a_b_s[j][:, t_in].T, g_a, axis=1)
            bias = jnp.stack([bias_p, bias_t])
            us, vs = _spatial(u, v, a_ln_g[j][None], a_ln_b[j][None], wmix, bias, n_prompt_tiles)
            chunk_v_s.append(vs[:m_samp].reshape(dec_batch, dec_seq, e_a))
            x, xb = _out_ln(us, a_w_out16, j, x, g1, b1, alpha, TM, m_prompt)
        elif kind == 1:
            nt = d // TN
            head_spec = pl.BlockSpec((TM, TN), lambda jj, i: (jnp.minimum(i, n_prompt_tiles - 1), jj))
            tail_spec = pl.BlockSpec((TM, TN), lambda jj, i: (jnp.maximum(i - n_prompt_tiles, 0), jj))
            vt_spec = pl.BlockSpec((TN, TM), lambda jj, i: (jj, jnp.minimum(i, n_prompt_tiles - 1)))
            full16 = jax.ShapeDtypeStruct((mp, d), BF16)
            vt16_shape = jax.ShapeDtypeStruct((d, m_prompt), BF16)
            head32 = jax.ShapeDtypeStruct((m_prompt, d), F32)
            tail32 = jax.ShapeDtypeStruct((mp - m_prompt, d), F32)
            q16, k16, vt16, k32p, k32t, v32p, v32t = _proj(
                xb, [(b_w_in, j, 0), (b_w_in, j, nt), (b_w_in, j, 2 * nt)],
                _ep_qkv,
                [_tile_spec(), _tile_spec(), vt_spec, head_spec, tail_spec, head_spec, tail_spec],
                [full16, full16, vt16_shape, head32, tail32, head32, tail32], n_col_tiles=nt,
                m_full=m_prompt, name="b_in_qkv")
            w_f = b_w_in[j][:, 3 * d:]
            lf, c, c_lanes = _logf(xb, w_f, b_b_f[j][None, :], tri, n_prompt_tiles, tiles_per_seq)
            scale = dh ** -0.5
            c_t = c.T
            o_prompt = _fox_prompt(q16, k16, vt16, c_t[:, None, :m_prompt], c_lanes, batch, seq, scale)

            cache_k_rows = cache_k.reshape(-1, dh)
            cache_v_rows = cache_v.reshape(-1, dh)
            cache_lf_rows = cache_logf.reshape(-1, H_B)
            ck_past = _suffix(page_table, cache_lf_rows, j, n_pool, tri_upper)
            qpad = ((0, 0), (0, 0), (0, Q_ROWS - dec_seq), (0, 0))
            q_s = q16[m_prompt:m_real].reshape(dec_batch, dec_seq, H_B, dh).transpose(0, 2, 1, 3)
            q_s = jnp.pad(q_s, qpad).reshape(dec_batch, H_B * Q_ROWS, dh)
            c_s = c[m_prompt:m_real].reshape(dec_batch, dec_seq, H_B, 1).transpose(0, 2, 1, 3)
            cq_s = jnp.pad(c_s, qpad).reshape(dec_batch, H_B * Q_ROWS, 1)
            o_samp = _fox_decode(page_table, q_s, cq_s, k32t, v32t, c_t[:, m_prompt:m_prompt + CHUNK], ck_past,
                                 cache_k_rows, cache_v_rows, j, n_pool, scale, dec_seq)
            o_samp = o_samp.reshape(dec_batch, H_B, Q_ROWS, dh)[:, :, :dec_seq].transpose(0, 2, 1, 3)
            o_samp = o_samp.reshape(m_samp, d).astype(BF16)
            o_all = jnp.concatenate([o_prompt, o_samp, jnp.zeros((mp - m_real, d), BF16)], axis=0)
            x, xb = _out_ln(o_all, b_w_o16, j, x, g1, b1, alpha, TM, m_prompt)
            k_p.append(k32p.reshape(batch, seq, H_B, dh))
            v_p.append(v32p.reshape(batch, seq, H_B, dh))
            lf_p.append(lf[:m_prompt].reshape(batch, seq, H_B))
            k_s.append(k32t[:m_samp].reshape(dec_batch, dec_seq, H_B, dh))
            v_s.append(v32t[:m_samp].reshape(dec_batch, dec_seq, H_B, dh))
            lf_s.append(lf[m_prompt:m_real].reshape(dec_batch, dec_seq, H_B))
        else:
            nt = e_c // TN
            st = state_conv[j]
            p1 = jnp.zeros((dec_batch, dec_seq, e_c), F32).at[:, 0].set(st[:, 1])
            p2 = jnp.zeros((dec_batch, dec_seq, e_c), F32).at[:, 0].set(st[:, 0]).at[:, 1].set(st[:, 1])
            pad = jnp.zeros((TM - m_samp, e_c), F32)
            p1 = jnp.concatenate([p1.reshape(m_samp, e_c), pad], axis=0)
            p2 = jnp.concatenate([p2.reshape(m_samp, e_c), pad], axis=0)
            n_tiles = mp // TM
            ep = functools.partial(_ep_conv, tiles_per_seq=tiles_per_seq, dec_seq=dec_seq)
            bgy, ztail, zs = _proj(
                xb, [(c_w_in, j, 0), (c_w_in, j, nt), (c_w_in, j, 2 * nt)], ep,
                [_tile_spec(), pl.BlockSpec((SUBLANES, TN), lambda jj, i: (i, jj)),
                 pl.BlockSpec((TM, TN), lambda jj, i: (0, jj))],
                [jax.ShapeDtypeStruct((mp, e_c), BF16), jax.ShapeDtypeStruct((n_tiles * SUBLANES, e_c), F32),
                 jax.ShapeDtypeStruct((TM, e_c), F32)],
                n_col_tiles=nt, m_full=m_prompt, name="c_in_conv",
                extra_in=[c_conv_w, p1, p2],
                extra_specs=[pl.BlockSpec((None, CONV_W, TN), lambda jj, i: (j, 0, jj)),
                             pl.BlockSpec((TM, TN), lambda jj, i: (0, jj)),
                             pl.BlockSpec((TM, TN), lambda jj, i: (0, jj))],
                extra_scratch=[pltpu.VMEM((SUBLANES, TN), F32)])
            zt = ztail.reshape(n_tiles, SUBLANES, e_c)
            last_tiles = jnp.arange(batch) * tiles_per_seq + tiles_per_seq - 1
            conv_p.append(zt[last_tiles][:, SUBLANES - (CONV_W - 1):, :])
            conv_s.append(zs[:m_samp].reshape(dec_batch, dec_seq, e_c)[:, dec_seq - (CONV_W - 1):, :])
            x, xb = _out_ln(bgy, c_w_out16, j, x, g1, b1, alpha, TM, m_prompt)

        f = li // 2
        if li % 2 == 0:
            nt = ffn_w_gate.shape[-1] // TN
            (h,) = _proj(xb, [(ffn_w_gate, f, 0), (ffn_w_up, f, 0)], _ep_swiglu, [_tile_spec(tm=TM_WIDE)],
                         [jax.ShapeDtypeStruct((mp, ffn_w_gate.shape[-1]), BF16)], n_col_tiles=nt,
                         m_full=m_prompt, name="ffn_up", tm=TM_WIDE)
            x, xb = _out_ln(h, ffn_w_down16, f, x, g2, b2, alpha, TM_WIDE_K, m_prompt)
        else:
            last = li == depth - 1
            x, xb = _moe(x, m_real, moe_w_router, moe_b_router[f][None], moe_w_gate, moe_w_up, moe_w_down, f,
                         g2, b2, alpha, m_prompt if last else None)

    if isinstance(x, tuple):
        x_head, x_tail = x
    else:
        x_head, x_tail = x[:m_prompt], x[m_prompt:]
    y_prompt = x_head.reshape(batch, seq, d)
    y_sample = x_tail[:m_samp].reshape(dec_batch, dec_seq, d)
    return (y_prompt, y_sample, jnp.stack(k_p), jnp.stack(v_p), jnp.stack(lf_p), jnp.stack(k_s),
            jnp.stack(v_s), jnp.stack(lf_s), jnp.stack(conv_p), jnp.stack(conv_s), jnp.stack(chunk_v_s))
```
